```python
import jax, jax.numpy as jnp
from jax import lax
import numpy as np

D_MODEL = 2048
BATCH = 4
SEQ = 4096
DEPTH = 1

CHUNK = 64
MIX_WIDTH = D_MODEL
HG_HEADS = 8
HG_HEAD_DIM = 128
HG_WIDTH = HG_HEADS * HG_HEAD_DIM
ATT_HEADS = 8
ATT_HEAD_DIM = 128
ATT_WIDTH = ATT_HEADS * ATT_HEAD_DIM
LEFT_CHUNKS = 8
BAND = (LEFT_CHUNKS + 1) * CHUNK
REL_CLIP = 128
D_FF = 5632
CONV_WIDTH = 3
PLE_DIM = 256
EPS = 1e-6
IN_SPLITS = (HG_WIDTH, HG_WIDTH, HG_WIDTH, HG_WIDTH, ATT_WIDTH, ATT_WIDTH, ATT_WIDTH)
IN_COLS = sum(IN_SPLITS)

kernel_name = "hybrid_hgrn2_chunkattn_convffn_ple"


def rmsnorm(x, g):
    xf = x.astype(jnp.float32)
    y = xf * lax.rsqrt(jnp.mean(xf * xf, axis=-1, keepdims=True) + EPS)
    return (y * g.astype(jnp.float32)).astype(x.dtype)


def hgrn2_mixer(q, f_pre, i_in, g, lb, norm_g):
    B, S, _ = q.shape
    nc = S // CHUNK
    lb = lb.astype(jnp.float32)
    sig = jax.nn.sigmoid(f_pre.astype(jnp.float32))
    f = lb + (1.0 - lb) * sig
    log_f = jnp.log(f)
    k = (1.0 - lb) * jax.nn.sigmoid(-f_pre.astype(jnp.float32))
    qf = jax.nn.silu(q.astype(jnp.float32))
    vf = i_in.astype(jnp.float32)

    def heads(t):
        return t.reshape(B, nc, CHUNK, HG_HEADS, HG_HEAD_DIM).transpose(1, 0, 3, 2, 4)

    causal = jnp.tril(jnp.ones((CHUNK, CHUNK), dtype=bool))

    def step(state, inp):
        qc, kc, vc, lfc = inp
        b = jnp.cumsum(lfc, axis=2)
        diff = b[:, :, :, None, :] - b[:, :, None, :, :]
        decay = jnp.exp(jnp.where(causal[None, None, :, :, None], diff, -jnp.inf))
        scores = jnp.einsum('bhtd,bhsd,bhtsd->bhts', qc, kc, decay)
        o_intra = jnp.einsum('bhts,bhsv->bhtv', scores, vc)
        o_inter = jnp.einsum('bhtd,bhdv->bhtv', qc * jnp.exp(b), state)
        b_last = b[:, :, -1:, :]
        new_state = (jnp.exp(b_last[:, :, 0, :])[..., None] * state
                     + jnp.einsum('bhsd,bhsv->bhdv', kc * jnp.exp(b_last - b), vc))
        return new_state, o_intra + o_inter

    s0 = jnp.zeros((B, HG_HEADS, HG_HEAD_DIM, HG_HEAD_DIM), jnp.float32)
    _, o = lax.scan(step, s0, (heads(qf), heads(k), heads(vf), heads(log_f)))
    o = o.transpose(1, 0, 3, 2, 4).reshape(B, S, HG_HEADS, HG_HEAD_DIM)
    o = rmsnorm(o, norm_g)
    gate = jax.nn.silu(g.astype(jnp.float32)).reshape(B, S, HG_HEADS, HG_HEAD_DIM)
    return (o * gate).reshape(B, S, HG_WIDTH).astype(q.dtype)


def chunk_attention(q, k, v, rel_bias):
    B, S, _ = q.shape
    nc = S // CHUNK
    pad = LEFT_CHUNKS * CHUNK

    def heads(t):
        return t.reshape(B, S, ATT_HEADS, ATT_HEAD_DIM).transpose(0, 2, 1, 3)

    qh = heads(q) * (ATT_HEAD_DIM ** -0.5)
    kp = jnp.pad(heads(k), ((0, 0), (0, 0), (pad, 0), (0, 0)))
    vp = jnp.pad(heads(v), ((0, 0), (0, 0), (pad, 0), (0, 0)))
    t_off = jnp.arange(CHUNK)[:, None]
    j_off = jnp.arange(BAND)[None, :]
    rel = t_off + pad - j_off
    bias = rel_bias.astype(jnp.float32)[:, jnp.clip(rel, -REL_CLIP, REL_CLIP) + REL_CLIP]
    band_idx = jnp.arange(BAND)

    def one_chunk(c):
        start = c * CHUNK
        qc = lax.dynamic_slice_in_dim(qh, start, CHUNK, axis=2)
        kc = lax.dynamic_slice_in_dim(kp, start, BAND, axis=2)
        vc = lax.dynamic_slice_in_dim(vp, start, BAND, axis=2)
        s = jnp.einsum('bhtd,bhsd->bhts', qc, kc).astype(jnp.float32) + bias[None]
        valid = (start - pad + band_idx) >= 0
        s = jnp.where(valid[None, None, None, :], s, jnp.finfo(jnp.float32).min)
        pr = jax.nn.softmax(s, axis=-1)
        return jnp.einsum('bhts,bhsd->bhtd', pr.astype(vc.dtype), vc)

    out = lax.map(one_chunk, jnp.arange(nc))
    return out.transpose(1, 0, 3, 2, 4).reshape(B, S, ATT_WIDTH)


def conv_ffn(h, w_up, conv_w, conv_b, w_down):
    S = h.shape[1]
    u = h @ w_up
    up = jnp.pad(u, ((0, 0), (CONV_WIDTH - 1, 0), (0, 0)))
    uc = conv_b + sum(conv_w[j] * up[:, j:j + S] for j in range(CONV_WIDTH))
    gate, val = uc[..., :D_FF], uc[..., D_FF:]
    return (jax.nn.silu(gate) * val) @ w_down


def setup_inputs(seed: int = 0) -> dict:
    key = jax.random.key(seed)
    ks = jax.random.split(key, 20)
    f32 = jnp.float32
    nrm = lambda k, shape, s: jax.random.normal(k, shape, f32) * s
    return {
        "x": nrm(ks[0], (BATCH, SEQ, D_MODEL), 1.0),
        "p": nrm(ks[1], (DEPTH, BATCH, SEQ, PLE_DIM), 1.0),
        "norm_mix": 1.0 + nrm(ks[2], (DEPTH, D_MODEL), 0.02),
        "w_in": nrm(ks[3], (DEPTH, D_MODEL, IN_COLS), D_MODEL ** -0.5),
        "lb_logits": nrm(ks[4], (DEPTH + 1, HG_WIDTH), 0.5),
        "hg_norm": 1.0 + nrm(ks[5], (DEPTH, HG_HEAD_DIM), 0.02),
        "rel_bias": nrm(ks[6], (DEPTH, ATT_HEADS, 2 * REL_CLIP + 1), 0.5),
        "w_out": nrm(ks[7], (DEPTH, MIX_WIDTH, D_MODEL), MIX_WIDTH ** -0.5),
        "norm_ffn": 1.0 + nrm(ks[8], (DEPTH, D_MODEL), 0.02),
        "w_up": nrm(ks[9], (DEPTH, D_MODEL, 2 * D_FF), D_MODEL ** -0.5),
        "conv_w": nrm(ks[10], (DEPTH, CONV_WIDTH, 2 * D_FF), CONV_WIDTH ** -0.5),
        "conv_b": nrm(ks[11], (DEPTH, 2 * D_FF), 0.02),
        "w_down": nrm(ks[12], (DEPTH, D_FF, D_MODEL), D_FF ** -0.5),
        "norm_ple": 1.0 + nrm(ks[13], (DEPTH, D_MODEL), 0.02),
        "w_ple_gate": nrm(ks[14], (DEPTH, D_MODEL, D_MODEL), D_MODEL ** -0.5),
        "w_ple_proj": nrm(ks[15], (DEPTH, PLE_DIM, D_MODEL), PLE_DIM ** -0.5),
        "final_norm": 1.0 + nrm(ks[16], (D_MODEL,), 0.02),
    }


def reference(x, p, norm_mix, w_in, lb_logits, hg_norm, rel_bias, w_out, norm_ffn,
              w_up, conv_w, conv_b, w_down, norm_ple, w_ple_gate, w_ple_proj, final_norm):
    lb_all = jnp.cumsum(jax.nn.softmax(lb_logits.astype(jnp.float32), axis=0), axis=0)[:DEPTH]
    split_idx = [int(v) for v in np.cumsum(IN_SPLITS)[:-1]]
    h = x
    for i in range(DEPTH):
        a = rmsnorm(h, norm_mix[i])
        proj = a @ w_in[i]
        hq, hf, hi, hg, aq, ak, av = jnp.split(proj, split_idx, axis=-1)
        y_hg = hgrn2_mixer(hq, hf, hi, hg, lb_all[i], hg_norm[i])
        y_att = chunk_attention(aq, ak, av, rel_bias[i])
        h = h + jnp.concatenate([y_hg, y_att], axis=-1) @ w_out[i]
        h = h + conv_ffn(rmsnorm(h, norm_ffn[i]), w_up[i], conv_w[i], conv_b[i], w_down[i])
        gate = jax.nn.sigmoid(rmsnorm(h, norm_ple[i]) @ w_ple_gate[i])
        h = h + gate * (p[i] @ w_ple_proj[i])
    return rmsnorm(h, final_norm)
```

```python
import functools

import jax
import jax.numpy as jnp
from jax import lax
from jax.experimental import pallas as pl
from jax.experimental.pallas import tpu as pltpu

F32 = jnp.float32
BF16 = jnp.bfloat16

EPS = 1e-6
CHUNK = 64
HEADS = 8
HEAD_DIM = 128
GROUP_WIDTH = HEADS * HEAD_DIM
LEFT_CHUNKS = 8
REL_CLIP = 128
CONV_WIDTH = 3
Q_BLOCK = 2 * CHUNK
KEY_BLOCKS = LEFT_CHUNKS * CHUNK // Q_BLOCK + 1
MASKED = -1e30
BF16_ROWS = 16
VMEM_LIMIT_BYTES = 56 * 1024 * 1024

NT_DIMS = (((1,), (1,)), ((), ()))
TN_DIMS = (((0,), (0,)), ((), ()))


def _params(*semantics):
    return pltpu.CompilerParams(dimension_semantics=semantics,
                                vmem_limit_bytes=VMEM_LIMIT_BYTES)


def _rms_scale(x):
    return lax.rsqrt(jnp.mean(x * x, axis=-1, keepdims=True) + EPS)


def _norm_matmul_kernel(x_ref, g_ref, w_ref, o_ref, a_ref):
    @pl.when(pl.program_id(1) == 0)
    def _():
        x = x_ref[...]
        a_ref[...] = (x * _rms_scale(x) * g_ref[...]).astype(BF16)

    o_ref[...] = jnp.dot(a_ref[...], w_ref[...],
                         preferred_element_type=F32).astype(o_ref.dtype)


def _norm_matmul(x, g, w, out_dtype, tm, tn, name):
    t, d = x.shape
    n = w.shape[1]
    tm = min(tm, t)
    return pl.pallas_call(
        _norm_matmul_kernel,
        grid=(t // tm, n // tn),
        in_specs=[pl.BlockSpec((tm, d), lambda i, j: (i, 0)),
                  pl.BlockSpec((1, d), lambda i, j: (0, 0)),
                  pl.BlockSpec((d, tn), lambda i, j: (0, j))],
        out_specs=pl.BlockSpec((tm, tn), lambda i, j: (i, j)),
        out_shape=jax.ShapeDtypeStruct((t, n), out_dtype),
        scratch_shapes=[pltpu.VMEM((tm, d), BF16)],
        compiler_params=_params("parallel", "arbitrary"),
        name=name,
    )(x, g, w)


def _select_rows(level, on_set, on_clear, bit_masks):
    n = 1 << level
    if n >= 8:
        parts = [(on_set if j & 1 else on_clear)[j * n:(j + 1) * n]
                 for j in range(CHUNK // n)]
        return jnp.concatenate(parts, axis=0)
    return jnp.where(bit_masks[level], on_set, on_clear)


def _swap_row_blocks(level, x, bit_masks):
    n = 1 << level
    if n >= 8:
        parts = [x[(j ^ 1) * n:((j ^ 1) + 1) * n] for j in range(CHUNK // n)]
        return jnp.concatenate(parts, axis=0)
    return jnp.where(bit_masks[level], pltpu.roll(x, n, 0),
                     pltpu.roll(x, CHUNK - n, 0))


def _hgrn_chunk(q, f_pre, v, g, lb, norm_g, state_t, bit_masks, pair_level):
    sig = jax.nn.sigmoid(f_pre)
    log_f = jnp.log(lb + (1.0 - lb) * sig)
    k = (1.0 - lb) * (1.0 - sig)
    qf = q * jax.nn.sigmoid(q)
    zeros = jnp.zeros_like(log_f)

    seg = log_f
    tot = log_f
    scores = jnp.zeros((CHUNK, CHUNK), F32)
    for level in range(6):
        expo = _select_rows(level, seg, tot - seg, bit_masks)
        x = (jnp.exp(expo) * _select_rows(level, qf, k, bit_masks)).astype(BF16)
        s = lax.dot_general(x, x, NT_DIMS, preferred_element_type=F32)
        scores = scores + jnp.where(pair_level == level, s, 0.0)
        sibling = _swap_row_blocks(level, tot, bit_masks)
        seg = seg + _select_rows(level, sibling, zeros, bit_masks)
        tot = tot + sibling
    b = seg
    b_last = tot

    vb = v.astype(BF16)
    diag = jnp.sum(qf * k, axis=-1, keepdims=True)
    o = jnp.dot(scores.astype(BF16), vb, preferred_element_type=F32) + diag * v
    qb = (qf * jnp.exp(b)).astype(BF16)
    o = o + lax.dot_general(qb, state_t.astype(BF16), NT_DIMS,
                            preferred_element_type=F32)
    k_end = (k * jnp.exp(b_last - b)).astype(BF16)
    new_state_t = state_t * jnp.exp(b_last[0:1, :]) + lax.dot_general(
        vb, k_end, TN_DIMS, preferred_element_type=F32)

    y = o * _rms_scale(o) * norm_g * (g * jax.nn.sigmoid(g))
    return y, new_state_t


def _hgrn_kernel(q_ref, f_ref, i_ref, g_ref, lbl_ref, ng_ref, y_ref, st_ref, *,
                 layer, n_chunks):
    @pl.when(pl.program_id(1) == 0)
    def _():
        st_ref[...] = jnp.zeros_like(st_ref)

    lbl = lbl_ref[...]
    e = jnp.exp(lbl - jnp.max(lbl, axis=0, keepdims=True))
    lb_all = (jnp.sum(e[0:layer + 1, :], axis=0, keepdims=True)
              / jnp.sum(e, axis=0, keepdims=True))
    norm_g = ng_ref[...]

    rows = lax.broadcasted_iota(jnp.int32, (CHUNK, HEAD_DIM), 0)
    bit_masks = [((rows >> level) & 1) == 1 for level in range(3)]
    t = lax.broadcasted_iota(jnp.int32, (CHUNK, CHUNK), 0)
    s = lax.broadcasted_iota(jnp.int32, (CHUNK, CHUNK), 1)
    pair_level = jnp.where(t > s, 31 - lax.clz(t ^ s), -1)

    def chunk_body(c, carry):
        r0 = pl.multiple_of(c * CHUNK, CHUNK)
        for h in range(HEADS):
            cols = slice(h * HEAD_DIM, (h + 1) * HEAD_DIM)
            y, st = _hgrn_chunk(
                q_ref[pl.ds(r0, CHUNK), cols], f_ref[pl.ds(r0, CHUNK), cols],
                i_ref[pl.ds(r0, CHUNK), cols], g_ref[pl.ds(r0, CHUNK), cols],
                lb_all[:, cols], norm_g, st_ref[h], bit_masks, pair_level)
            st_ref[h] = st
            y_ref[pl.ds(r0, CHUNK), cols] = y.astype(y_ref.dtype)
        return carry

    lax.fori_loop(0, n_chunks, chunk_body, 0)


def _hgrn(proj, lb_logits, norm_g, layer, batch, seq, rows_per_step):
    t = proj.shape[0]
    tb = min(rows_per_step, seq)
    steps = seq // tb
    blk = lambda col: pl.BlockSpec((tb, GROUP_WIDTH),
                                   lambda b, s: (b * steps + s, col))
    return pl.pallas_call(
        functools.partial(_hgrn_kernel, layer=layer, n_chunks=tb // CHUNK),
        grid=(batch, steps),
        in_specs=[blk(0), blk(1), blk(2), blk(3),
                  pl.BlockSpec(lb_logits.shape, lambda b, s: (0, 0)),
                  pl.BlockSpec((1, HEAD_DIM), lambda b, s: (0, 0))],
        out_specs=pl.BlockSpec((tb, GROUP_WIDTH), lambda b, s: (b * steps + s, 0)),
        out_shape=jax.ShapeDtypeStruct((t, GROUP_WIDTH), BF16),
        scratch_shapes=[pltpu.VMEM((HEADS, HEAD_DIM, HEAD_DIM), F32)],
        compiler_params=_params("arbitrary", "arbitrary"),
        name="hgrn2",
    )(proj, proj, proj, proj, lb_logits, norm_g)


def _attn_kernel(q_ref, *refs, scale):
    k_refs = refs[:KEY_BLOCKS]
    v_refs = refs[KEY_BLOCKS:2 * KEY_BLOCKS]
    bias_ref, o_ref = refs[2 * KEY_BLOCKS:]
    i = pl.program_id(1)
    penalty = [jnp.where(i - (KEY_BLOCKS - 1) + kb >= 0, 0.0, MASKED).astype(F32)
               for kb in range(KEY_BLOCKS)]
    for h in range(HEADS):
        cols = slice(h * HEAD_DIM, (h + 1) * HEAD_DIM)
        q = (q_ref[:, cols].astype(F32) * scale).astype(BF16)
        scores = []
        for kb in range(KEY_BLOCKS):
            s = lax.dot_general(q, k_refs[kb][:, cols], NT_DIMS,
                                preferred_element_type=F32)
            s = s + bias_ref[h, :, kb * Q_BLOCK:(kb + 1) * Q_BLOCK] + penalty[kb]
            scores.append(s)
        m = functools.reduce(jnp.maximum,
                             [jnp.max(s, axis=-1, keepdims=True) for s in scores])
        probs = [jnp.exp(s - m) for s in scores]
        denom = functools.reduce(
            jnp.add, [jnp.sum(p, axis=-1, keepdims=True) for p in probs])
        o = functools.reduce(jnp.add, [
            jnp.dot(p.astype(BF16), v_refs[kb][:, cols], preferred_element_type=F32)
            for kb, p in enumerate(probs)])
        o_ref[:, cols] = (o / denom).astype(o_ref.dtype)


def _band_bias(rel_bias):
    tq = jnp.arange(Q_BLOCK)[:, None]
    tk = jnp.arange(KEY_BLOCKS * Q_BLOCK)[None, :]
    rel = tq + (KEY_BLOCKS - 1) * Q_BLOCK - tk
    bias = rel_bias.astype(F32)[:, jnp.clip(rel, -REL_CLIP, REL_CLIP) + REL_CLIP]
    q_chunk = tq // CHUNK
    k_chunk = tk // CHUNK
    in_band = (k_chunk >= q_chunk) & (k_chunk <= q_chunk + LEFT_CHUNKS)
    return jnp.where(in_band[None], bias, MASKED)


def _attention(proj, rel_bias, batch, seq):
    t = proj.shape[0]
    steps = seq // Q_BLOCK

    def kv_spec(kb, col):
        return pl.BlockSpec(
            (Q_BLOCK, GROUP_WIDTH),
            lambda b, i: (b * steps + jnp.maximum(i - (KEY_BLOCKS - 1) + kb, 0), col))

    bias = _band_bias(rel_bias)
    return pl.pallas_call(
        functools.partial(_attn_kernel, scale=HEAD_DIM ** -0.5),
        grid=(batch, steps),
        in_specs=([pl.BlockSpec((Q_BLOCK, GROUP_WIDTH), lambda b, i: (b * steps + i, 0))]
                  + [kv_spec(kb, 1) for kb in range(KEY_BLOCKS)]
                  + [kv_spec(kb, 2) for kb in range(KEY_BLOCKS)]
                  + [pl.BlockSpec(bias.shape, lambda b, i: (0, 0, 0))]),
        out_specs=pl.BlockSpec((Q_BLOCK, GROUP_WIDTH), lambda b, i: (b * steps + i, 0)),
        out_shape=jax.ShapeDtypeStruct((t, GROUP_WIDTH), BF16),
        compiler_params=_params("parallel", "arbitrary"),
        name="chunk_attn",
    )(proj, *([proj] * (2 * KEY_BLOCKS)), bias)


def _out_proj_kernel(yh_ref, ya_ref, w_ref, x_ref, g_ref, h_ref, a_ref):
    mix = (jnp.dot(yh_ref[...], w_ref[0:GROUP_WIDTH, :], preferred_element_type=F32)
           + jnp.dot(ya_ref[...], w_ref[GROUP_WIDTH:2 * GROUP_WIDTH, :],
                     preferred_element_type=F32))
    h = x_ref[...] + mix
    h_ref[...] = h
    a_ref[...] = (h * _rms_scale(h) * g_ref[...]).astype(BF16)


def _out_proj(y_hg, y_att, w, x, g, tm):
    t, d = x.shape
    tm = min(tm, t)
    rows = lambda width: pl.BlockSpec((tm, width), lambda i: (i, 0))
    return pl.pallas_call(
        _out_proj_kernel,
        grid=(t // tm,),
        in_specs=[rows(GROUP_WIDTH), rows(GROUP_WIDTH),
                  pl.BlockSpec(w.shape, lambda i: (0, 0)),
                  rows(d), pl.BlockSpec((1, d), lambda i: (0, 0))],
        out_specs=[rows(d), rows(d)],
        out_shape=[jax.ShapeDtypeStruct((t, d), F32),
                   jax.ShapeDtypeStruct((t, d), BF16)],
        compiler_params=_params("parallel"),
        name="out_proj",
    )(y_hg, y_att, w, x, g)


def _ffn_up_kernel(a_ref, halo_ref, wg_ref, wv_ref, cwg_ref, cwv_ref, cbg_ref, cbv_ref,
                   o_ref, lhs_ref, ug_ref, uv_ref, *, blocks_per_seq, tm):
    @pl.when(pl.program_id(1) == 0)
    def _():
        first = (pl.program_id(0) % blocks_per_seq) == 0
        halo = halo_ref[...]
        lhs_ref[0:BF16_ROWS, :] = jnp.where(first, jnp.zeros_like(halo), halo)
        lhs_ref[BF16_ROWS:, :] = a_ref[...]

    lhs = lhs_ref[...]
    ug_ref[...] = jnp.dot(lhs, wg_ref[...], preferred_element_type=F32)
    uv_ref[...] = jnp.dot(lhs, wv_ref[...], preferred_element_type=F32)

    def causal_conv(u_ref, cw_ref, cb_ref):
        out = cb_ref[...]
        for j in range(CONV_WIDTH):
            shift = BF16_ROWS - (CONV_WIDTH - 1) + j
            out = out + cw_ref[j:j + 1, :] * u_ref[pl.ds(shift, tm), :]
        return out

    gate = causal_conv(ug_ref, cwg_ref, cbg_ref)
    val = causal_conv(uv_ref, cwv_ref, cbv_ref)
    o_ref[...] = (gate * jax.nn.sigmoid(gate) * val).astype(o_ref.dtype)


def _ffn_up(a, w_up, conv_w, conv_b, seq, tm, tn):
    t, d = a.shape
    d_ff = w_up.shape[1] // 2
    tm = min(tm, seq)
    n_tiles = d_ff // tn
    halo_blocks = tm // BF16_ROWS
    gate_cols = lambda rows: pl.BlockSpec((rows, tn), lambda i, j: (0, j))
    val_cols = lambda rows: pl.BlockSpec((rows, tn), lambda i, j: (0, n_tiles + j))
    return pl.pallas_call(
        functools.partial(_ffn_up_kernel, blocks_per_seq=seq // tm, tm=tm),
        grid=(t // tm, n_tiles),
        in_specs=[pl.BlockSpec((tm, d), lambda i, j: (i, 0)),
                  pl.BlockSpec((BF16_ROWS, d),
                               lambda i, j: (jnp.maximum(i * halo_blocks - 1, 0), 0)),
                  gate_cols(d), val_cols(d),
                  gate_cols(CONV_WIDTH), val_cols(CONV_WIDTH),
                  gate_cols(1), val_cols(1)],
        out_specs=pl.BlockSpec((tm, tn), lambda i, j: (i, j)),
        out_shape=jax.ShapeDtypeStruct((t, d_ff), BF16),
        scratch_shapes=[pltpu.VMEM((tm + BF16_ROWS, d), BF16),
                        pltpu.VMEM((tm + BF16_ROWS, tn), F32),
                        pltpu.VMEM((tm + BF16_ROWS, tn), F32)],
        compiler_params=_params("parallel", "arbitrary"),
        name="ffn_up",
    )(a, a, w_up, w_up, conv_w, conv_w, conv_b, conv_b)


def _ffn_down_kernel(act_ref, w_ref, h_ref, o_ref):
    o_ref[...] = h_ref[...] + jnp.dot(act_ref[...], w_ref[...],
                                      preferred_element_type=F32)


def _ffn_down(act, w, h, tm, tn):
    t, d_ff = act.shape
    d = w.shape[1]
    tm = min(tm, t)
    return pl.pallas_call(
        _ffn_down_kernel,
        grid=(t // tm, d // tn),
        in_specs=[pl.BlockSpec((tm, d_ff), lambda i, j: (i, 0)),
                  pl.BlockSpec((d_ff, tn), lambda i, j: (0, j)),
                  pl.BlockSpec((tm, tn), lambda i, j: (i, j))],
        out_specs=pl.BlockSpec((tm, tn), lambda i, j: (i, j)),
        out_shape=jax.ShapeDtypeStruct((t, d), F32),
        compiler_params=_params("parallel", "arbitrary"),
        name="ffn_down",
    )(act, w, h)


def _ple_kernel(h_ref, p_ref, g_ref, wg_ref, wp_ref, fn_ref, o_ref, *, last_layer):
    h = h_ref[...]
    a = (h * _rms_scale(h) * g_ref[...]).astype(BF16)
    gate = jax.nn.sigmoid(jnp.dot(a, wg_ref[...], preferred_element_type=F32))
    emb = jnp.dot(p_ref[...].astype(BF16), wp_ref[...], preferred_element_type=F32)
    h = h + gate * emb
    if last_layer:
        h = h * _rms_scale(h) * fn_ref[...]
    o_ref[...] = h


def _ple_final(h, p, g, w_gate, w_proj, final_g, last_layer, tm):
    t, d = h.shape
    tm = min(tm, t)
    whole = lambda arr: pl.BlockSpec(arr.shape, lambda i: (0, 0))
    return pl.pallas_call(
        functools.partial(_ple_kernel, last_layer=last_layer),
        grid=(t // tm,),
        in_specs=[pl.BlockSpec((tm, d), lambda i: (i, 0)),
                  pl.BlockSpec((tm, p.shape[1]), lambda i: (i, 0)),
                  whole(g), whole(w_gate), whole(w_proj), whole(final_g)],
        out_specs=pl.BlockSpec((tm, d), lambda i: (i, 0)),
        out_shape=jax.ShapeDtypeStruct((t, d), F32),
        compiler_params=_params("parallel"),
        name="ple_final",
    )(h, p, g, w_gate, w_proj, final_g)


def kernel(x, p, norm_mix, w_in, lb_logits, hg_norm, rel_bias, w_out, norm_ffn, w_up,
           conv_w, conv_b, w_down, norm_ple, w_ple_gate, w_ple_proj, final_norm):
    batch, seq, d = x.shape
    depth = w_in.shape[0]
    t = batch * seq
    hg_cols = 4 * GROUP_WIDTH
    row = lambda vec: vec.reshape(1, -1)

    h = x.reshape(t, d)
    for layer in range(depth):
        w_in_l = w_in[layer].astype(BF16)
        proj_hg = _norm_matmul(h, row(norm_mix[layer]), w_in_l[:, :hg_cols], F32,
                               tm=1024, tn=512, name="in_proj_hgrn")
        proj_att = _norm_matmul(h, row(norm_mix[layer]), w_in_l[:, hg_cols:], BF16,
                                tm=1024, tn=512, name="in_proj_attn")
        y_hg = _hgrn(proj_hg, lb_logits, row(hg_norm[layer]), layer, batch, seq,
                     rows_per_step=256)
        y_att = _attention(proj_att, rel_bias[layer], batch, seq)
        h, a = _out_proj(y_hg, y_att, w_out[layer].astype(BF16), h,
                         row(norm_ffn[layer]), tm=512)
        act = _ffn_up(a, w_up[layer].astype(BF16), conv_w[layer], row(conv_b[layer]),
                      seq, tm=512, tn=512)
        h = _ffn_down(act, w_down[layer].astype(BF16), h, tm=1024, tn=512)
        h = _ple_final(h, p[layer].reshape(t, -1), row(norm_ple[layer]),
                       w_ple_gate[layer].astype(BF16), w_ple_proj[layer].astype(BF16),
                       row(final_norm), last_layer=layer == depth - 1, tm=512)
    return h.reshape(batch, seq, d)
```

```python
import functools

import jax
import jax.numpy as jnp
from jax import lax
from jax.experimental import pallas as pl
from jax.experimental.pallas import tpu as pltpu

F32 = jnp.float32
BF16 = jnp.bfloat16

EPS = 1e-6
CHUNK = 64
HEADS = 8
HEAD_DIM = 128
GROUP_WIDTH = HEADS * HEAD_DIM
LEFT_CHUNKS = 8
REL_CLIP = 128
CONV_WIDTH = 3
Q_BLOCK = 2 * CHUNK
KEY_BLOCKS = LEFT_CHUNKS * CHUNK // Q_BLOCK + 1
MASKED = -1e30
BF16_ROWS = 16
VMEM_LIMIT_BYTES = 56 * 1024 * 1024

NT_DIMS = (((1,), (1,)), ((), ()))
TN_DIMS = (((0,), (0,)), ((), ()))


def _params(*semantics):
    return pltpu.CompilerParams(dimension_semantics=semantics,
                                vmem_limit_bytes=VMEM_LIMIT_BYTES)


def _rms_scale(x):
    return lax.rsqrt(jnp.mean(x * x, axis=-1, keepdims=True) + EPS)


def _in_proj_kernel(x_ref, g_ref, w_ref, hg_ref, att_ref, a_ref, *, hg_tiles):
    j = pl.program_id(1)

    @pl.when(j == 0)
    def _():
        x = x_ref[...]
        a_ref[...] = (x * _rms_scale(x) * g_ref[...]).astype(BF16)

    out = jnp.dot(a_ref[...], w_ref[...], preferred_element_type=F32)

    @pl.when(j < hg_tiles)
    def _():
        hg_ref[...] = out

    @pl.when(j >= hg_tiles)
    def _():
        att_ref[...] = out.astype(att_ref.dtype)


def _in_proj(x, g, w, hg_cols, tm, tn):
    t, d = x.shape
    n = w.shape[1]
    tm = min(tm, t)
    hg_tiles = hg_cols // tn
    return pl.pallas_call(
        functools.partial(_in_proj_kernel, hg_tiles=hg_tiles),
        grid=(t // tm, n // tn),
        in_specs=[pl.BlockSpec((tm, d), lambda i, j: (i, 0)),
                  pl.BlockSpec((1, d), lambda i, j: (0, 0)),
                  pl.BlockSpec((d, tn), lambda i, j: (0, j))],
        out_specs=[pl.BlockSpec((tm, tn), lambda i, j: (i, jnp.minimum(j, hg_tiles - 1))),
                   pl.BlockSpec((tm, tn), lambda i, j: (i, jnp.maximum(j - hg_tiles, 0)))],
        out_shape=[jax.ShapeDtypeStruct((t, hg_cols), F32),
                   jax.ShapeDtypeStruct((t, n - hg_cols), BF16)],
        scratch_shapes=[pltpu.VMEM((tm, d), BF16)],
        compiler_params=_params("parallel", "arbitrary"),
        name="in_proj",
    )(x, g, w)


def _select_rows(level, on_set, on_clear, bit_masks):
    n = 1 << level
    if n >= 8:
        parts = [(on_set if j & 1 else on_clear)[j * n:(j + 1) * n]
                 for j in range(CHUNK // n)]
        return jnp.concatenate(parts, axis=0)
    return jnp.where(bit_masks[level], on_set, on_clear)


def _swap_row_blocks(level, x, bit_masks):
    n = 1 << level
    if n >= 8:
        parts = [x[(j ^ 1) * n:((j ^ 1) + 1) * n] for j in range(CHUNK // n)]
        return jnp.concatenate(parts, axis=0)
    return jnp.where(bit_masks[level], pltpu.roll(x, n, 0),
                     pltpu.roll(x, CHUNK - n, 0))


def _hgrn_chunk(q, f_pre, v, g, lb, norm_g, state_t, bit_masks, pair_level):
    sig = jax.nn.sigmoid(f_pre)
    log_f = jnp.log(lb + (1.0 - lb) * sig)
    k = (1.0 - lb) * (1.0 - sig)
    qf = q * jax.nn.sigmoid(q)
    zeros = jnp.zeros_like(log_f)

    seg = log_f
    tot = log_f
    scores = jnp.zeros((CHUNK, CHUNK), F32)
    for level in range(6):
        expo = _select_rows(level, seg, tot - seg, bit_masks)
        x = (jnp.exp(expo) * _select_rows(level, qf, k, bit_masks)).astype(BF16)
        s = lax.dot_general(x, x, NT_DIMS, preferred_element_type=F32)
        scores = scores + jnp.where(pair_level == level, s, 0.0)
        sibling = _swap_row_blocks(level, tot, bit_masks)
        seg = seg + _select_rows(level, sibling, zeros, bit_masks)
        tot = tot + sibling
    b = seg
    b_last = tot

    vb = v.astype(BF16)
    diag = jnp.sum(qf * k, axis=-1, keepdims=True)
    o = jnp.dot(scores.astype(BF16), vb, preferred_element_type=F32) + diag * v
    qb = (qf * jnp.exp(b)).astype(BF16)
    o = o + lax.dot_general(qb, state_t.astype(BF16), NT_DIMS,
                            preferred_element_type=F32)
    k_end = (k * jnp.exp(b_last - b)).astype(BF16)
    new_state_t = state_t * jnp.exp(b_last[0:1, :]) + lax.dot_general(
        vb, k_end, TN_DIMS, preferred_element_type=F32)

    y = o * _rms_scale(o) * norm_g * (g * jax.nn.sigmoid(g))
    return y, new_state_t


def _hgrn_kernel(q_ref, f_ref, i_ref, g_ref, lbl_ref, ng_ref, y_ref, st_ref, *,
                 layer, n_chunks):
    @pl.when(pl.program_id(1) == 0)
    def _():
        st_ref[...] = jnp.zeros_like(st_ref)

    lbl = lbl_ref[...]
    e = jnp.exp(lbl - jnp.max(lbl, axis=0, keepdims=True))
    lb_all = (jnp.sum(e[0:layer + 1, :], axis=0, keepdims=True)
              / jnp.sum(e, axis=0, keepdims=True))
    norm_g = ng_ref[...]

    rows = lax.broadcasted_iota(jnp.int32, (CHUNK, HEAD_DIM), 0)
    bit_masks = [((rows >> level) & 1) == 1 for level in range(3)]
    t = lax.broadcasted_iota(jnp.int32, (CHUNK, CHUNK), 0)
    s = lax.broadcasted_iota(jnp.int32, (CHUNK, CHUNK), 1)
    pair_level = jnp.where(t > s, 31 - lax.clz(t ^ s), -1)

    def chunk_body(c, carry):
        r0 = pl.multiple_of(c * CHUNK, CHUNK)
        for h in range(HEADS):
            cols = slice(h * HEAD_DIM, (h + 1) * HEAD_DIM)
            y, st = _hgrn_chunk(
                q_ref[pl.ds(r0, CHUNK), cols], f_ref[pl.ds(r0, CHUNK), cols],
                i_ref[pl.ds(r0, CHUNK), cols], g_ref[pl.ds(r0, CHUNK), cols],
                lb_all[:, cols], norm_g, st_ref[h], bit_masks, pair_level)
            st_ref[h] = st
            y_ref[pl.ds(r0, CHUNK), cols] = y.astype(y_ref.dtype)
        return carry

    lax.fori_loop(0, n_chunks, chunk_body, 0)


def _hgrn(proj, lb_logits, norm_g, layer, batch, seq, rows_per_step):
    t = proj.shape[0]
    tb = min(rows_per_step, seq)
    steps = seq // tb
    blk = lambda col: pl.BlockSpec((tb, GROUP_WIDTH),
                                   lambda b, s: (b * steps + s, col))
    return pl.pallas_call(
        functools.partial(_hgrn_kernel, layer=layer, n_chunks=tb // CHUNK),
        grid=(batch, steps),
        in_specs=[blk(0), blk(1), blk(2), blk(3),
                  pl.BlockSpec(lb_logits.shape, lambda b, s: (0, 0)),
                  pl.BlockSpec((1, HEAD_DIM), lambda b, s: (0, 0))],
        out_specs=pl.BlockSpec((tb, GROUP_WIDTH), lambda b, s: (b * steps + s, 0)),
        out_shape=jax.ShapeDtypeStruct((t, GROUP_WIDTH), BF16),
        scratch_shapes=[pltpu.VMEM((HEADS, HEAD_DIM, HEAD_DIM), F32)],
        compiler_params=_params("arbitrary", "arbitrary"),
        name="hgrn2",
    )(proj, proj, proj, proj, lb_logits, norm_g)


def _attn_kernel(q_ref, *refs, scale):
    k_refs = refs[:KEY_BLOCKS]
    v_refs = refs[KEY_BLOCKS:2 * KEY_BLOCKS]
    bias_ref, o_ref = refs[2 * KEY_BLOCKS:]
    i = pl.program_id(1)
    penalty = jnp.concatenate(
        [jnp.full((1, Q_BLOCK), jnp.where(i - (KEY_BLOCKS - 1) + kb >= 0, 0.0, MASKED), F32)
         for kb in range(KEY_BLOCKS)], axis=1)
    head_cols = [slice(h * HEAD_DIM, (h + 1) * HEAD_DIM) for h in range(HEADS)]
    scores = []
    for h, cols in enumerate(head_cols):
        q = (q_ref[:, cols].astype(F32) * scale).astype(BF16)
        k_band = jnp.concatenate([k_ref[:, cols] for k_ref in k_refs], axis=0)
        s = lax.dot_general(q, k_band, NT_DIMS, preferred_element_type=F32)
        scores.append(s + (bias_ref[h] + penalty))
    probs, denoms = [], []
    for s in scores:
        p = jnp.exp(s - jnp.max(s, axis=-1, keepdims=True))
        denoms.append(jnp.sum(p, axis=-1, keepdims=True))
        probs.append(p.astype(BF16))
    for h, cols in enumerate(head_cols):
        v_band = jnp.concatenate([v_ref[:, cols] for v_ref in v_refs], axis=0)
        o = jnp.dot(probs[h], v_band, preferred_element_type=F32)
        o_ref[:, cols] = (o / denoms[h]).astype(o_ref.dtype)


def _band_bias_kernel(tab_ref, o_ref):
    band = KEY_BLOCKS * Q_BLOCK
    tq = lax.broadcasted_iota(jnp.int32, (Q_BLOCK, band), 0)
    tk = lax.broadcasted_iota(jnp.int32, (Q_BLOCK, band), 1)
    q_chunk = tq // CHUNK
    k_chunk = tk // CHUNK
    in_band = (k_chunk >= q_chunk) & (k_chunk <= q_chunk + LEFT_CHUNKS)
    for h in range(HEADS):
        rows = jnp.broadcast_to(tab_ref[h:h + 1, :], (Q_BLOCK, band + Q_BLOCK))
        toeplitz = pltpu.roll(rows, 0, 1, stride=1, stride_axis=0)
        o_ref[h] = jnp.where(in_band, toeplitz[:, Q_BLOCK:], MASKED)


def _band_bias(rel_bias):
    band = KEY_BLOCKS * Q_BLOCK
    offset = jnp.arange(band + Q_BLOCK) - Q_BLOCK
    rel = (KEY_BLOCKS - 1) * Q_BLOCK - offset
    table = rel_bias.astype(F32)[:, jnp.clip(rel, -REL_CLIP, REL_CLIP) + REL_CLIP]
    return pl.pallas_call(
        _band_bias_kernel,
        out_shape=jax.ShapeDtypeStruct((HEADS, Q_BLOCK, band), F32),
        name="band_bias",
    )(table)


def _attention(proj, rel_bias, batch, seq):
    t = proj.shape[0]
    steps = seq // Q_BLOCK

    def kv_spec(kb, col):
        return pl.BlockSpec(
            (Q_BLOCK, GROUP_WIDTH),
            lambda b, i: (b * steps + jnp.maximum(i - (KEY_BLOCKS - 1) + kb, 0), col))

    bias = _band_bias(rel_bias)
    return pl.pallas_call(
        functools.partial(_attn_kernel, scale=HEAD_DIM ** -0.5),
        grid=(batch, steps),
        in_specs=([pl.BlockSpec((Q_BLOCK, GROUP_WIDTH), lambda b, i: (b * steps + i, 0))]
                  + [kv_spec(kb, 1) for kb in range(KEY_BLOCKS)]
                  + [kv_spec(kb, 2) for kb in range(KEY_BLOCKS)]
                  + [pl.BlockSpec(bias.shape, lambda b, i: (0, 0, 0))]),
        out_specs=pl.BlockSpec((Q_BLOCK, GROUP_WIDTH), lambda b, i: (b * steps + i, 0)),
        out_shape=jax.ShapeDtypeStruct((t, GROUP_WIDTH), BF16),
        compiler_params=_params("parallel", "arbitrary"),
        name="chunk_attn",
    )(proj, *([proj] * (2 * KEY_BLOCKS)), bias)


def _out_proj_kernel(yh_ref, ya_ref, w_ref, x_ref, g_ref, h_ref, a_ref):
    mix = (jnp.dot(yh_ref[...], w_ref[0:GROUP_WIDTH, :], preferred_element_type=F32)
           + jnp.dot(ya_ref[...], w_ref[GROUP_WIDTH:2 * GROUP_WIDTH, :],
                     preferred_element_type=F32))
    h = x_ref[...] + mix
    h_ref[...] = h
    a_ref[...] = (h * _rms_scale(h) * g_ref[...]).astype(BF16)


def _out_proj(y_hg, y_att, w, x, g, tm):
    t, d = x.shape
    tm = min(tm, t)
    rows = lambda width: pl.BlockSpec((tm, width), lambda i: (i, 0))
    return pl.pallas_call(
        _out_proj_kernel,
        grid=(t // tm,),
        in_specs=[rows(GROUP_WIDTH), rows(GROUP_WIDTH),
                  pl.BlockSpec(w.shape, lambda i: (0, 0)),
                  rows(d), pl.BlockSpec((1, d), lambda i: (0, 0))],
        out_specs=[rows(d), rows(d)],
        out_shape=[jax.ShapeDtypeStruct((t, d), F32),
                   jax.ShapeDtypeStruct((t, d), BF16)],
        compiler_params=_params("parallel"),
        name="out_proj",
    )(y_hg, y_att, w, x, g)


def _ffn_up_kernel(a_ref, halo_ref, wg_ref, wv_ref, cwg_ref, cwv_ref, cbg_ref, cbv_ref,
                   o_ref, lhs_ref, ug_ref, uv_ref, *, blocks_per_seq, tm):
    @pl.when(pl.program_id(1) == 0)
    def _():
        first = (pl.program_id(0) % blocks_per_seq) == 0
        halo = halo_ref[...]
        lhs_ref[0:BF16_ROWS, :] = jnp.where(first, jnp.zeros_like(halo), halo)
        lhs_ref[BF16_ROWS:, :] = a_ref[...]

    lhs = lhs_ref[...]
    ug_ref[...] = jnp.dot(lhs, wg_ref[...], preferred_element_type=F32)
    uv_ref[...] = jnp.dot(lhs, wv_ref[...], preferred_element_type=F32)

    def causal_conv(u_ref, cw_ref, cb_ref):
        out = cb_ref[...]
        for j in range(CONV_WIDTH):
            shift = BF16_ROWS - (CONV_WIDTH - 1) + j
            out = out + cw_ref[j:j + 1, :] * u_ref[pl.ds(shift, tm), :]
        return out

    gate = causal_conv(ug_ref, cwg_ref, cbg_ref)
    val = causal_conv(uv_ref, cwv_ref, cbv_ref)
    o_ref[...] = (gate * jax.nn.sigmoid(gate) * val).astype(o_ref.dtype)


def _ffn_up(a, w_up, conv_w, conv_b, seq, tm, tn):
    t, d = a.shape
    d_ff = w_up.shape[1] // 2
    tm = min(tm, seq)
    n_tiles = d_ff // tn
    halo_blocks = tm // BF16_ROWS
    gate_cols = lambda rows: pl.BlockSpec((rows, tn), lambda i, j: (0, j))
    val_cols = lambda rows: pl.BlockSpec((rows, tn), lambda i, j: (0, n_tiles + j))
    return pl.pallas_call(
        functools.partial(_ffn_up_kernel, blocks_per_seq=seq // tm, tm=tm),
        grid=(t // tm, n_tiles),
        in_specs=[pl.BlockSpec((tm, d), lambda i, j: (i, 0)),
                  pl.BlockSpec((BF16_ROWS, d),
                               lambda i, j: (jnp.maximum(i * halo_blocks - 1, 0), 0)),
                  gate_cols(d), val_cols(d),
                  gate_cols(CONV_WIDTH), val_cols(CONV_WIDTH),
                  gate_cols(1), val_cols(1)],
        out_specs=pl.BlockSpec((tm, tn), lambda i, j: (i, j)),
        out_shape=jax.ShapeDtypeStruct((t, d_ff), BF16),
        scratch_shapes=[pltpu.VMEM((tm + BF16_ROWS, d), BF16),
                        pltpu.VMEM((tm + BF16_ROWS, tn), F32),
                        pltpu.VMEM((tm + BF16_ROWS, tn), F32)],
        compiler_params=_params("parallel", "arbitrary"),
        name="ffn_up",
    )(a, a, w_up, w_up, conv_w, conv_w, conv_b, conv_b)


def _ffn_down_kernel(act_ref, w_ref, h_ref, o_ref):
    o_ref[...] = h_ref[...] + jnp.dot(act_ref[...], w_ref[...],
                                      preferred_element_type=F32)


def _ffn_down(act, w, h, tm, tn):
    t, d_ff = act.shape
    d = w.shape[1]
    tm = min(tm, t)
    return pl.pallas_call(
        _ffn_down_kernel,
        grid=(t // tm, d // tn),
        in_specs=[pl.BlockSpec((tm, d_ff), lambda i, j: (i, 0)),
                  pl.BlockSpec((d_ff, tn), lambda i, j: (0, j)),
                  pl.BlockSpec((tm, tn), lambda i, j: (i, j))],
        out_specs=pl.BlockSpec((tm, tn), lambda i, j: (i, j)),
        out_shape=jax.ShapeDtypeStruct((t, d), F32),
        compiler_params=_params("parallel", "arbitrary"),
        name="ffn_down",
    )(act, w, h)


def _ple_kernel(h_ref, p_ref, g_ref, wg_ref, wp_ref, fn_ref, o_ref, *, last_layer):
    h = h_ref[...]
    a = (h * _rms_scale(h) * g_ref[...]).astype(BF16)
    gate = jax.nn.sigmoid(jnp.dot(a, wg_ref[...], preferred_element_type=F32))
    emb = jnp.dot(p_ref[...].astype(BF16), wp_ref[...], preferred_element_type=F32)
    h = h + gate * emb
    if last_layer:
        h = h * _rms_scale(h) * fn_ref[...]
    o_ref[...] = h


def _ple_final(h, p, g, w_gate, w_proj, final_g, last_layer, tm):
    t, d = h.shape
    tm = min(tm, t)
    whole = lambda arr: pl.BlockSpec(arr.shape, lambda i: (0, 0))
    return pl.pallas_call(
        functools.partial(_ple_kernel, last_layer=last_layer),
        grid=(t // tm,),
        in_specs=[pl.BlockSpec((tm, d), lambda i: (i, 0)),
                  pl.BlockSpec((tm, p.shape[1]), lambda i: (i, 0)),
                  whole(g), whole(w_gate), whole(w_proj), whole(final_g)],
        out_specs=pl.BlockSpec((tm, d), lambda i: (i, 0)),
        out_shape=jax.ShapeDtypeStruct((t, d), F32),
        compiler_params=_params("parallel"),
        name="ple_final",
    )(h, p, g, w_gate, w_proj, final_g)


def kernel(x, p, norm_mix, w_in, lb_logits, hg_norm, rel_bias, w_out, norm_ffn, w_up,
           conv_w, conv_b, w_down, norm_ple, w_ple_gate, w_ple_proj, final_norm):
    batch, seq, d = x.shape
    depth = w_in.shape[0]
    t = batch * seq
    hg_cols = 4 * GROUP_WIDTH
    row = lambda vec: vec.reshape(1, -1)

    h = x.reshape(t, d)
    for layer in range(depth):
        proj_hg, proj_att = _in_proj(h, row(norm_mix[layer]), w_in[layer].astype(BF16),
                                     hg_cols, tm=1024, tn=512)
        y_hg = _hgrn(proj_hg, lb_logits, row(hg_norm[layer]), layer, batch, seq,
                     rows_per_step=256)
        y_att = _attention(proj_att, rel_bias[layer], batch, seq)
        h, a = _out_proj(y_hg, y_att, w_out[layer].astype(BF16), h,
                         row(norm_ffn[layer]), tm=512)
        act = _ffn_up(a, w_up[layer].astype(BF16), conv_w[layer], row(conv_b[layer]),
                      seq, tm=512, tn=512)
        h = _ffn_down(act, w_down[layer].astype(BF16), h, tm=1024, tn=512)
        h = _ple_final(h, p[layer].reshape(t, -1), row(norm_ple[layer]),
                       w_ple_gate[layer].astype(BF16), w_ple_proj[layer].astype(BF16),
                       row(final_norm), last_layer=layer == depth - 1, tm=512)
    return h.reshape(batch, seq, d)
```

```python
import functools

import jax
import jax.numpy as jnp
from jax import lax
from jax.experimental import pallas as pl
from jax.experimental.pallas import tpu as pltpu

F32 = jnp.float32
BF16 = jnp.bfloat16

EPS = 1e-6
CHUNK = 64
HEADS = 8
HEAD_DIM = 128
GROUP_WIDTH = HEADS * HEAD_DIM
LEFT_CHUNKS = 8
REL_CLIP = 128
CONV_WIDTH = 3
Q_BLOCK = 2 * CHUNK
KEY_BLOCKS = LEFT_CHUNKS * CHUNK // Q_BLOCK + 1
MASKED = -1e30
HALO = 8
LANES = 128
EPILOGUE_ROWS = 32
VMEM_LIMIT_BYTES = 56 * 1024 * 1024

NT_DIMS = (((1,), (1,)), ((), ()))
TN_DIMS = (((0,), (0,)), ((), ()))


def _params(*semantics, flags=None):
    return pltpu.CompilerParams(dimension_semantics=semantics,
                                vmem_limit_bytes=VMEM_LIMIT_BYTES, flags=flags)


def _rms_scale(x):
    return lax.rsqrt(jnp.mean(x * x, axis=-1, keepdims=True) + EPS)


def _in_proj_kernel(x_ref, g_ref, w_ref, hg_ref, att_ref, a_ref, *, hg_tiles):
    j = pl.program_id(1)

    @pl.when(j == 0)
    def _():
        x = x_ref[...]
        a_ref[...] = (x * _rms_scale(x) * g_ref[...]).astype(BF16)

    out = jnp.dot(a_ref[...], w_ref[...], preferred_element_type=F32)

    @pl.when(j < hg_tiles)
    def _():
        hg_ref[...] = out

    @pl.when(j >= hg_tiles)
    def _():
        att_ref[...] = out.astype(att_ref.dtype)


def _in_proj(x, g, w, hg_cols, tm, tn):
    t, d = x.shape
    n = w.shape[1]
    tm = min(tm, t)
    hg_tiles = hg_cols // tn
    return pl.pallas_call(
        functools.partial(_in_proj_kernel, hg_tiles=hg_tiles),
        grid=(t // tm, n // tn),
        in_specs=[pl.BlockSpec((tm, d), lambda i, j: (i, 0)),
                  pl.BlockSpec((1, d), lambda i, j: (0, 0)),
                  pl.BlockSpec((d, tn), lambda i, j: (0, j))],
        out_specs=[pl.BlockSpec((tm, tn), lambda i, j: (i, jnp.minimum(j, hg_tiles - 1))),
                   pl.BlockSpec((tm, tn), lambda i, j: (i, jnp.maximum(j - hg_tiles, 0)))],
        out_shape=[jax.ShapeDtypeStruct((t, hg_cols), F32),
                   jax.ShapeDtypeStruct((t, n - hg_cols), BF16)],
        scratch_shapes=[pltpu.VMEM((tm, d), BF16)],
        compiler_params=_params("parallel", "arbitrary"),
        name="in_proj",
    )(x, g, w)


def _select_rows(level, on_set, on_clear, bit_masks):
    n = 1 << level
    if n >= 8:
        parts = [(on_set if j & 1 else on_clear)[j * n:(j + 1) * n]
                 for j in range(CHUNK // n)]
        return jnp.concatenate(parts, axis=0)
    return jnp.where(bit_masks[level], on_set, on_clear)


def _swap_row_blocks(level, x, bit_masks):
    n = 1 << level
    if n >= 8:
        parts = [x[(j ^ 1) * n:((j ^ 1) + 1) * n] for j in range(CHUNK // n)]
        return jnp.concatenate(parts, axis=0)
    return jnp.where(bit_masks[level], pltpu.roll(x, n, 0),
                     pltpu.roll(x, CHUNK - n, 0))


def _hgrn_chunk(q, f_pre, v, g, lb, norm_g, state_t, bit_masks, pair_level):
    sig = jax.nn.sigmoid(f_pre)
    log_f = jnp.log(lb + (1.0 - lb) * sig)
    k = (1.0 - lb) * (1.0 - sig)
    qf = q * jax.nn.sigmoid(q)
    zeros = jnp.zeros_like(log_f)

    seg = log_f
    tot = log_f
    scores = jnp.zeros((CHUNK, CHUNK), F32)
    for level in range(6):
        expo = _select_rows(level, seg, tot - seg, bit_masks)
        x = (jnp.exp(expo) * _select_rows(level, qf, k, bit_masks)).astype(BF16)
        s = lax.dot_general(x, x, NT_DIMS, preferred_element_type=F32)
        scores = scores + jnp.where(pair_level == level, s, 0.0)
        sibling = _swap_row_blocks(level, tot, bit_masks)
        seg = seg + _select_rows(level, sibling, zeros, bit_masks)
        tot = tot + sibling
    b = seg
    b_last = tot

    vb = v.astype(BF16)
    diag = jnp.sum(qf * k, axis=-1, keepdims=True)
    o = jnp.dot(scores.astype(BF16), vb, preferred_element_type=F32) + diag * v
    qb = (qf * jnp.exp(b)).astype(BF16)
    o = o + lax.dot_general(qb, state_t.astype(BF16), NT_DIMS,
                            preferred_element_type=F32)
    k_end = (k * jnp.exp(b_last - b)).astype(BF16)
    new_state_t = state_t * jnp.exp(b_last[0:1, :]) + lax.dot_general(
        vb, k_end, TN_DIMS, preferred_element_type=F32)

    y = o * _rms_scale(o) * norm_g * (g * jax.nn.sigmoid(g))
    return y, new_state_t


def _hgrn_kernel(q_ref, f_ref, i_ref, g_ref, lbl_ref, ng_ref, y_ref, st_ref, *,
                 layer, n_chunks):
    @pl.when(pl.program_id(1) == 0)
    def _():
        st_ref[...] = jnp.zeros_like(st_ref)

    lbl = lbl_ref[...]
    e = jnp.exp(lbl - jnp.max(lbl, axis=0, keepdims=True))
    lb_all = (jnp.sum(e[0:layer + 1, :], axis=0, keepdims=True)
              / jnp.sum(e, axis=0, keepdims=True))
    norm_g = ng_ref[...]

    rows = lax.broadcasted_iota(jnp.int32, (CHUNK, HEAD_DIM), 0)
    bit_masks = [((rows >> level) & 1) == 1 for level in range(3)]
    t = lax.broadcasted_iota(jnp.int32, (CHUNK, CHUNK), 0)
    s = lax.broadcasted_iota(jnp.int32, (CHUNK, CHUNK), 1)
    pair_level = jnp.where(t > s, 31 - lax.clz(t ^ s), -1)

    def chunk_body(c, carry):
        r0 = pl.multiple_of(c * CHUNK, CHUNK)
        for h in range(HEADS):
            cols = slice(h * HEAD_DIM, (h + 1) * HEAD_DIM)
            y, st = _hgrn_chunk(
                q_ref[pl.ds(r0, CHUNK), cols], f_ref[pl.ds(r0, CHUNK), cols],
                i_ref[pl.ds(r0, CHUNK), cols], g_ref[pl.ds(r0, CHUNK), cols],
                lb_all[:, cols], norm_g, st_ref[h], bit_masks, pair_level)
            st_ref[h] = st
            y_ref[pl.ds(r0, CHUNK), cols] = y.astype(y_ref.dtype)
        return carry

    lax.fori_loop(0, n_chunks, chunk_body, 0)


def _hgrn(proj, lb_logits, norm_g, layer, batch, seq, rows_per_step):
    t = proj.shape[0]
    tb = min(rows_per_step, seq)
    steps = seq // tb
    blk = lambda col: pl.BlockSpec((tb, GROUP_WIDTH),
                                   lambda b, s: (b * steps + s, col))
    return pl.pallas_call(
        functools.partial(_hgrn_kernel, layer=layer, n_chunks=tb // CHUNK),
        grid=(batch, steps),
        in_specs=[blk(0), blk(1), blk(2), blk(3),
                  pl.BlockSpec(lb_logits.shape, lambda b, s: (0, 0)),
                  pl.BlockSpec((1, HEAD_DIM), lambda b, s: (0, 0))],
        out_specs=pl.BlockSpec((tb, GROUP_WIDTH), lambda b, s: (b * steps + s, 0)),
        out_shape=jax.ShapeDtypeStruct((t, GROUP_WIDTH), BF16),
        scratch_shapes=[pltpu.VMEM((HEADS, HEAD_DIM, HEAD_DIM), F32)],
        compiler_params=_params("arbitrary", "arbitrary"),
        name="hgrn2",
    )(proj, proj, proj, proj, lb_logits, norm_g)


def _attn_kernel(q_ref, *refs, scale):
    k_refs = refs[:KEY_BLOCKS]
    v_refs = refs[KEY_BLOCKS:2 * KEY_BLOCKS]
    bias_ref, o_ref = refs[2 * KEY_BLOCKS:]
    i = pl.program_id(1)
    penalty = jnp.concatenate(
        [jnp.full((1, Q_BLOCK), jnp.where(i - (KEY_BLOCKS - 1) + kb >= 0, 0.0, MASKED), F32)
         for kb in range(KEY_BLOCKS)], axis=1)
    head_cols = [slice(h * HEAD_DIM, (h + 1) * HEAD_DIM) for h in range(HEADS)]
    scores = []
    for h, cols in enumerate(head_cols):
        q = (q_ref[:, cols].astype(F32) * scale).astype(BF16)
        k_band = jnp.concatenate([k_ref[:, cols] for k_ref in k_refs], axis=0)
        s = lax.dot_general(q, k_band, NT_DIMS, preferred_element_type=F32)
        scores.append(s + (bias_ref[h] + penalty))
    probs, denoms = [], []
    for s in scores:
        p = jnp.exp(s - jnp.max(s, axis=-1, keepdims=True))
        denoms.append(jnp.sum(p, axis=-1, keepdims=True))
        probs.append(p.astype(BF16))
    for h, cols in enumerate(head_cols):
        v_band = jnp.concatenate([v_ref[:, cols] for v_ref in v_refs], axis=0)
        o = jnp.dot(probs[h], v_band, preferred_element_type=F32)
        o_ref[:, cols] = (o / denoms[h]).astype(o_ref.dtype)


def _band_bias_kernel(tab_ref, o_ref):
    band = KEY_BLOCKS * Q_BLOCK
    tq = lax.broadcasted_iota(jnp.int32, (Q_BLOCK, band), 0)
    tk = lax.broadcasted_iota(jnp.int32, (Q_BLOCK, band), 1)
    q_chunk = tq // CHUNK
    k_chunk = tk // CHUNK
    in_band = (k_chunk >= q_chunk) & (k_chunk <= q_chunk + LEFT_CHUNKS)
    for h in range(HEADS):
        rows = jnp.broadcast_to(tab_ref[h:h + 1, :], (Q_BLOCK, band + Q_BLOCK))
        toeplitz = pltpu.roll(rows, 0, 1, stride=1, stride_axis=0)
        o_ref[h] = jnp.where(in_band, toeplitz[:, Q_BLOCK:], MASKED)


def _band_bias(rel_bias):
    band = KEY_BLOCKS * Q_BLOCK
    offset = jnp.arange(band + Q_BLOCK) - Q_BLOCK
    rel = (KEY_BLOCKS - 1) * Q_BLOCK - offset
    table = rel_bias.astype(F32)[:, jnp.clip(rel, -REL_CLIP, REL_CLIP) + REL_CLIP]
    return pl.pallas_call(
        _band_bias_kernel,
        out_shape=jax.ShapeDtypeStruct((HEADS, Q_BLOCK, band), F32),
        name="band_bias",
    )(table)


def _attention(proj, rel_bias, batch, seq):
    t = proj.shape[0]
    steps = seq // Q_BLOCK

    def kv_spec(kb, col):
        return pl.BlockSpec(
            (Q_BLOCK, GROUP_WIDTH),
            lambda b, i: (b * steps + jnp.maximum(i - (KEY_BLOCKS - 1) + kb, 0), col))

    bias = _band_bias(rel_bias)
    return pl.pallas_call(
        functools.partial(_attn_kernel, scale=HEAD_DIM ** -0.5),
        grid=(batch, steps),
        in_specs=([pl.BlockSpec((Q_BLOCK, GROUP_WIDTH), lambda b, i: (b * steps + i, 0))]
                  + [kv_spec(kb, 1) for kb in range(KEY_BLOCKS)]
                  + [kv_spec(kb, 2) for kb in range(KEY_BLOCKS)]
                  + [pl.BlockSpec(bias.shape, lambda b, i: (0, 0, 0))]),
        out_specs=pl.BlockSpec((Q_BLOCK, GROUP_WIDTH), lambda b, i: (b * steps + i, 0)),
        out_shape=jax.ShapeDtypeStruct((t, GROUP_WIDTH), BF16),
        compiler_params=_params("parallel", "arbitrary"),
        name="chunk_attn",
    )(proj, *([proj] * (2 * KEY_BLOCKS)), bias)


def _out_proj_kernel(yh_ref, ya_ref, w_ref, x_ref, g_ref, h_ref, a_ref):
    mix = (jnp.dot(yh_ref[...], w_ref[0:GROUP_WIDTH, :], preferred_element_type=F32)
           + jnp.dot(ya_ref[...], w_ref[GROUP_WIDTH:2 * GROUP_WIDTH, :],
                     preferred_element_type=F32))
    h = x_ref[...] + mix
    h_ref[...] = h
    a_ref[...] = (h * _rms_scale(h) * g_ref[...]).astype(BF16)


def _out_proj(y_hg, y_att, w, x, g, tm):
    t, d = x.shape
    tm = min(tm, t)
    rows = lambda width: pl.BlockSpec((tm, width), lambda i: (i, 0))
    return pl.pallas_call(
        _out_proj_kernel,
        grid=(t // tm,),
        in_specs=[rows(GROUP_WIDTH), rows(GROUP_WIDTH),
                  pl.BlockSpec(w.shape, lambda i: (0, 0)),
                  rows(d), pl.BlockSpec((1, d), lambda i: (0, 0))],
        out_specs=[rows(d), rows(d)],
        out_shape=[jax.ShapeDtypeStruct((t, d), F32),
                   jax.ShapeDtypeStruct((t, d), BF16)],
        compiler_params=_params("parallel"),
        name="out_proj",
    )(y_hg, y_att, w, x, g)


def _ffn_up_kernel(a_ref, wg_ref, wv_ref, cwg_ref, cwv_ref, cbg_ref, cbv_ref, o_ref,
                   ug0_ref, uv0_ref, ug1_ref, uv1_ref, *,
                   n_tiles, row_blocks, blocks_per_seq, tm):
    s = pl.program_id(0)
    slabs = ug0_ref.shape[0]

    @pl.when(s == 0)
    def _():
        ug1_ref[...] = jnp.zeros_like(ug1_ref)
        uv1_ref[...] = jnp.zeros_like(uv1_ref)

    def step(new, old):
        row_block = jnp.minimum(s, n_tiles * row_blocks - 1) % row_blocks
        first = (row_block % blocks_per_seq) == 0
        a = a_ref[...]
        for u_new, u_old, w_ref in ((new[0], old[0], wg_ref), (new[1], old[1], wv_ref)):
            u = jnp.dot(a, w_ref[...], preferred_element_type=F32)
            for k in range(slabs):
                tail = u_old[k, pl.ds(tm, HALO), :]
                u_new[k, 0:HALO, :] = jnp.where(first, jnp.zeros_like(tail), tail)
                u_new[k, pl.ds(HALO, tm), :] = u[:, k * LANES:(k + 1) * LANES]

        zero = jnp.zeros((EPILOGUE_ROWS, LANES), F32)
        for k in range(slabs):
            lanes = slice(k * LANES, (k + 1) * LANES)
            for r in range(0, tm, EPILOGUE_ROWS):
                def causal_conv(u_ref, cw_ref, cb_ref, out):
                    for j in range(CONV_WIDTH):
                        start = r + HALO - (CONV_WIDTH - 1) + j
                        out = out + cw_ref[j:j + 1, lanes] * u_ref[
                            k, pl.ds(start, EPILOGUE_ROWS, stride=1), :]
                    return out

                gate = causal_conv(old[0], cwg_ref, cbg_ref, cbg_ref[:, lanes] + zero)
                val = causal_conv(old[1], cwv_ref, cbv_ref, cbv_ref[:, lanes])
                act = gate * jax.nn.sigmoid(gate) * val
                o_ref[r:r + EPILOGUE_ROWS, lanes] = act.astype(o_ref.dtype)
                bits = lax.bitcast_convert_type(act, jnp.uint32)
                zero = lax.bitcast_convert_type((bits >> 16) >> 16, F32)

    @pl.when(s % 2 == 0)
    def _():
        step((ug0_ref, uv0_ref), (ug1_ref, uv1_ref))

    @pl.when(s % 2 == 1)
    def _():
        step((ug1_ref, uv1_ref), (ug0_ref, uv0_ref))


def _ffn_up(a, w_up, conv_w, conv_b, seq, tm, tn):
    t, d = a.shape
    d_ff = w_up.shape[1] // 2
    tm = min(tm, seq)
    n_tiles = d_ff // tn
    row_blocks = t // tm
    last = n_tiles * row_blocks - 1
    cur = lambda s: jnp.minimum(s, last)
    prev = lambda s: jnp.maximum(s - 1, 0)
    cur_gate = lambda rows: pl.BlockSpec((rows, tn), lambda s: (0, cur(s) // row_blocks))
    cur_val = lambda rows: pl.BlockSpec((rows, tn),
                                        lambda s: (0, n_tiles + cur(s) // row_blocks))
    prev_gate = lambda rows: pl.BlockSpec((rows, tn), lambda s: (0, prev(s) // row_blocks))
    prev_val = lambda rows: pl.BlockSpec((rows, tn),
                                         lambda s: (0, n_tiles + prev(s) // row_blocks))
    u_buffer = pltpu.VMEM((tn // LANES, tm + HALO, LANES), F32)
    return pl.pallas_call(
        functools.partial(_ffn_up_kernel, n_tiles=n_tiles, row_blocks=row_blocks,
                          blocks_per_seq=seq // tm, tm=tm),
        grid=(last + 2,),
        in_specs=[pl.BlockSpec((tm, d), lambda s: (cur(s) % row_blocks, 0)),
                  cur_gate(d), cur_val(d),
                  prev_gate(CONV_WIDTH), prev_val(CONV_WIDTH),
                  prev_gate(1), prev_val(1)],
        out_specs=pl.BlockSpec((tm, tn),
                               lambda s: (prev(s) % row_blocks, prev(s) // row_blocks)),
        out_shape=jax.ShapeDtypeStruct((t, d_ff), BF16),
        scratch_shapes=[u_buffer, u_buffer, u_buffer, u_buffer],
        compiler_params=_params("arbitrary"),
        name="ffn_up",
    )(a, w_up, w_up, conv_w, conv_w, conv_b, conv_b)


def _ffn_down_kernel(act_ref, w_ref, h_ref, o_ref):
    o_ref[...] = h_ref[...] + jnp.dot(act_ref[...], w_ref[...],
                                      preferred_element_type=F32)


def _ffn_down(act, w, h, tm, tn):
    t, d_ff = act.shape
    d = w.shape[1]
    tm = min(tm, t)
    return pl.pallas_call(
        _ffn_down_kernel,
        grid=(t // tm, d // tn),
        in_specs=[pl.BlockSpec((tm, d_ff), lambda i, j: (i, 0)),
                  pl.BlockSpec((d_ff, tn), lambda i, j: (0, j)),
                  pl.BlockSpec((tm, tn), lambda i, j: (i, j))],
        out_specs=pl.BlockSpec((tm, tn), lambda i, j: (i, j)),
        out_shape=jax.ShapeDtypeStruct((t, d), F32),
        compiler_params=_params("parallel", "arbitrary"),
        name="ffn_down",
    )(act, w, h)


def _ple_kernel(h_ref, p_ref, g_ref, wg_ref, wp_ref, fn_ref, o_ref, *, last_layer):
    h = h_ref[...]
    a = (h * _rms_scale(h) * g_ref[...]).astype(BF16)
    gate = jax.nn.sigmoid(jnp.dot(a, wg_ref[...], preferred_element_type=F32))
    emb = jnp.dot(p_ref[...].astype(BF16), wp_ref[...], preferred_element_type=F32)
    h = h + gate * emb
    if last_layer:
        h = h * _rms_scale(h) * fn_ref[...]
    o_ref[...] = h


def _ple_final(h, p, g, w_gate, w_proj, final_g, last_layer, tm):
    t, d = h.shape
    tm = min(tm, t)
    whole = lambda arr: pl.BlockSpec(arr.shape, lambda i: (0, 0))
    return pl.pallas_call(
        functools.partial(_ple_kernel, last_layer=last_layer),
        grid=(t // tm,),
        in_specs=[pl.BlockSpec((tm, d), lambda i: (i, 0)),
                  pl.BlockSpec((tm, p.shape[1]), lambda i: (i, 0)),
                  whole(g), whole(w_gate), whole(w_proj), whole(final_g)],
        out_specs=pl.BlockSpec((tm, d), lambda i: (i, 0)),
        out_shape=jax.ShapeDtypeStruct((t, d), F32),
        compiler_params=_params("parallel"),
        name="ple_final",
    )(h, p, g, w_gate, w_proj, final_g)


def kernel(x, p, norm_mix, w_in, lb_logits, hg_norm, rel_bias, w_out, norm_ffn, w_up,
           conv_w, conv_b, w_down, norm_ple, w_ple_gate, w_ple_proj, final_norm):
    batch, seq, d = x.shape
    depth = w_in.shape[0]
    t = batch * seq
    hg_cols = 4 * GROUP_WIDTH
    row = lambda vec: vec.reshape(1, -1)

    h = x.reshape(t, d)
    for layer in range(depth):
        proj_hg, proj_att = _in_proj(h, row(norm_mix[layer]), w_in[layer].astype(BF16),
                                     hg_cols, tm=1024, tn=1024)
        y_hg = _hgrn(proj_hg, lb_logits, row(hg_norm[layer]), layer, batch, seq,
                     rows_per_step=256)
        y_att = _attention(proj_att, rel_bias[layer], batch, seq)
        h, a = _out_proj(y_hg, y_att, w_out[layer].astype(BF16), h,
                         row(norm_ffn[layer]), tm=512)
        act = _ffn_up(a, w_up[layer].astype(BF16), conv_w[layer], row(conv_b[layer]),
                      seq, tm=1024, tn=512)
        h = _ffn_down(act, w_down[layer].astype(BF16), h, tm=1024, tn=512)
        h = _ple_final(h, p[layer].reshape(t, -1), row(norm_ple[layer]),
                       w_ple_gate[layer].astype(BF16), w_ple_proj[layer].astype(BF16),
                       row(final_norm), last_layer=layer == depth - 1, tm=512)
    return h.reshape(batch, seq, d)
```

```python
import functools

import jax
import jax.numpy as jnp
from jax import lax
from jax.experimental import pallas as pl
from jax.experimental.pallas import tpu as pltpu

F32 = jnp.float32
BF16 = jnp.bfloat16

EPS = 1e-6
CHUNK = 64
HEADS = 8
HEAD_DIM = 128
GROUP_WIDTH = HEADS * HEAD_DIM
LEFT_CHUNKS = 8
REL_CLIP = 128
CONV_WIDTH = 3
Q_BLOCK = 2 * CHUNK
KEY_BLOCKS = LEFT_CHUNKS * CHUNK // Q_BLOCK + 1
MASKED = -1e30
HALO = 8
LANES = 128
LEVELS = 6
LOG2_E = 1.4426950408889634
EPILOGUE_ROWS = 32
VMEM_LIMIT_BYTES = 56 * 1024 * 1024

NT_DIMS = (((1,), (1,)), ((), ()))
TN_DIMS = (((0,), (0,)), ((), ()))


def _params(*semantics, flags=None):
    return pltpu.CompilerParams(dimension_semantics=semantics,
                                vmem_limit_bytes=VMEM_LIMIT_BYTES, flags=flags)


def _rms_scale(x):
    return lax.rsqrt(jnp.mean(x * x, axis=-1, keepdims=True) + EPS)


def _sigmoid(x):
    return 0.5 * jnp.tanh(0.5 * x) + 0.5


def _in_proj_kernel(x_ref, g_ref, w_ref, hg_ref, att_ref, a_ref, *, hg_tiles):
    j = pl.program_id(1)

    @pl.when(j == 0)
    def _():
        x = x_ref[...]
        a_ref[...] = (x * _rms_scale(x) * g_ref[...]).astype(BF16)

    out = jnp.dot(a_ref[...], w_ref[...], preferred_element_type=F32)

    @pl.when(j < hg_tiles)
    def _():
        hg_ref[...] = out

    @pl.when(j >= hg_tiles)
    def _():
        att_ref[...] = out.astype(att_ref.dtype)


def _in_proj(x, g, w, hg_cols, tm, tn):
    t, d = x.shape
    n = w.shape[1]
    tm = min(tm, t)
    hg_tiles = hg_cols // tn
    return pl.pallas_call(
        functools.partial(_in_proj_kernel, hg_tiles=hg_tiles),
        grid=(t // tm, n // tn),
        in_specs=[pl.BlockSpec((tm, d), lambda i, j: (i, 0)),
                  pl.BlockSpec((1, d), lambda i, j: (0, 0)),
                  pl.BlockSpec((d, tn), lambda i, j: (0, j))],
        out_specs=[pl.BlockSpec((tm, tn), lambda i, j: (i, jnp.minimum(j, hg_tiles - 1))),
                   pl.BlockSpec((tm, tn), lambda i, j: (i, jnp.maximum(j - hg_tiles, 0)))],
        out_shape=[jax.ShapeDtypeStruct((t, hg_cols), F32),
                   jax.ShapeDtypeStruct((t, n - hg_cols), BF16)],
        scratch_shapes=[pltpu.VMEM((tm, d), BF16)],
        compiler_params=_params("parallel", "arbitrary"),
        name="in_proj",
    )(x, g, w)


def _select_rows(level, on_set, on_clear, bit_masks):
    n = 1 << level
    if n >= 8:
        parts = [(on_set if (r >> level) & 1 else on_clear)[r:r + 8]
                 for r in range(0, CHUNK, 8)]
        return jnp.concatenate(parts, axis=0)
    return jnp.where(bit_masks[level], on_set, on_clear)


def _swap_row_blocks(level, x, bit_masks):
    n = 1 << level
    if n >= 8:
        parts = [x[r ^ n:(r ^ n) + 8] for r in range(0, CHUNK, 8)]
        return jnp.concatenate(parts, axis=0)
    tiles = x.reshape(CHUNK // 8, 8, x.shape[-1])
    down = pltpu.roll(tiles, n, 1).reshape(x.shape)
    if 2 * n == 8:
        return down
    up = pltpu.roll(tiles, 8 - n, 1).reshape(x.shape)
    return jnp.where(bit_masks[level], down, up)


def _hgrn_level_products(q, f_pre, lb, bit_masks):
    sig = _sigmoid(f_pre)
    log_f = jnp.log(lb + (1.0 - lb) * sig) * LOG2_E
    k = (1.0 - lb) * (1.0 - sig)
    qf = q * _sigmoid(q)
    zeros = jnp.zeros_like(log_f)

    seg = log_f
    tot = log_f
    products = []
    for level in range(LEVELS):
        expo = _select_rows(level, seg, tot - seg, bit_masks)
        x = (jnp.exp2(expo) * _select_rows(level, qf, k, bit_masks)).astype(BF16)
        products.append(lax.dot_general(x, x, NT_DIMS, preferred_element_type=F32))
        sibling = _swap_row_blocks(level, tot, bit_masks)
        seg = seg + _select_rows(level, sibling, zeros, bit_masks)
        tot = tot + sibling
    return qf, k, products, seg, tot


def _hgrn_scores(products, pair_level):
    tiles = []
    for r in range(CHUNK // 8):
        rows = slice(8 * r, 8 * r + 8)
        level_of = pair_level[rows]
        acc = jnp.zeros((8, CHUNK), F32)
        for level in range(LEVELS):
            if level >= 3 and not (r >> (level - 3)) & 1:
                continue
            acc = jnp.where(level_of == level, products[level][rows], acc)
        tiles.append(acc)
    return jnp.concatenate(tiles, axis=0)


def _hgrn_kernel(q_ref, f_ref, i_ref, g_ref, lbl_ref, ng_ref, y_ref, st_ref, *,
                 layer, n_chunks):
    @pl.when(pl.program_id(1) == 0)
    def _():
        st_ref[...] = jnp.zeros_like(st_ref)

    lbl = lbl_ref[...]
    e = jnp.exp(lbl - jnp.max(lbl, axis=0, keepdims=True))
    lb_all = (jnp.sum(e[0:layer + 1, :], axis=0, keepdims=True)
              / jnp.sum(e, axis=0, keepdims=True))
    norm_g = ng_ref[...]

    rows = lax.broadcasted_iota(jnp.int32, (CHUNK, HEAD_DIM), 0)
    bit_masks = [((rows >> level) & 1) == 1 for level in range(3)]
    t = lax.broadcasted_iota(jnp.int32, (CHUNK, CHUNK), 0)
    s = lax.broadcasted_iota(jnp.int32, (CHUNK, CHUNK), 1)
    pair_level = jnp.where(t > s, 31 - lax.clz(t ^ s), -1)
    head_cols = [slice(h * HEAD_DIM, (h + 1) * HEAD_DIM) for h in range(HEADS)]

    def chunk_body(c, carry):
        r0 = pl.multiple_of(c * CHUNK, CHUNK)
        chunk_rows = pl.ds(r0, CHUNK)
        phase1 = [_hgrn_level_products(q_ref[chunk_rows, cols], f_ref[chunk_rows, cols],
                                       lb_all[:, cols], bit_masks) for cols in head_cols]
        outs = []
        for h, cols in enumerate(head_cols):
            qf, k, products, b, b_last = phase1[h]
            v = i_ref[chunk_rows, cols]
            vb = v.astype(BF16)
            state_t = st_ref[h]
            scores = _hgrn_scores(products, pair_level)
            diag = jnp.sum(qf * k, axis=-1, keepdims=True)
            o = jnp.dot(scores.astype(BF16), vb, preferred_element_type=F32) + diag * v
            qb = (qf * jnp.exp2(b)).astype(BF16)
            o = o + lax.dot_general(qb, state_t.astype(BF16), NT_DIMS,
                                    preferred_element_type=F32)
            k_end = (k * jnp.exp2(b_last - b)).astype(BF16)
            st_ref[h] = state_t * jnp.exp2(b_last[0:1, :]) + lax.dot_general(
                vb, k_end, TN_DIMS, preferred_element_type=F32)
            outs.append(o)
        for h, cols in enumerate(head_cols):
            o = outs[h]
            g = g_ref[chunk_rows, cols]
            y = o * _rms_scale(o) * norm_g * (g * _sigmoid(g))
            y_ref[chunk_rows, cols] = y.astype(y_ref.dtype)
        return carry

    lax.fori_loop(0, n_chunks, chunk_body, 0)


def _hgrn(proj, lb_logits, norm_g, layer, batch, seq, rows_per_step):
    t = proj.shape[0]
    tb = min(rows_per_step, seq)
    steps = seq // tb
    blk = lambda col: pl.BlockSpec((tb, GROUP_WIDTH),
                                   lambda b, s: (b * steps + s, col))
    return pl.pallas_call(
        functools.partial(_hgrn_kernel, layer=layer, n_chunks=tb // CHUNK),
        grid=(batch, steps),
        in_specs=[blk(0), blk(1), blk(2), blk(3),
                  pl.BlockSpec(lb_logits.shape, lambda b, s: (0, 0)),
                  pl.BlockSpec((1, HEAD_DIM), lambda b, s: (0, 0))],
        out_specs=pl.BlockSpec((tb, GROUP_WIDTH), lambda b, s: (b * steps + s, 0)),
        out_shape=jax.ShapeDtypeStruct((t, GROUP_WIDTH), BF16),
        scratch_shapes=[pltpu.VMEM((HEADS, HEAD_DIM, HEAD_DIM), F32)],
        compiler_params=_params("arbitrary", "arbitrary"),
        name="hgrn2",
    )(proj, proj, proj, proj, lb_logits, norm_g)


def _attn_kernel(q_ref, *refs, scale):
    k_refs = refs[:KEY_BLOCKS]
    v_refs = refs[KEY_BLOCKS:2 * KEY_BLOCKS]
    bias_ref, o_ref = refs[2 * KEY_BLOCKS:]
    i = pl.program_id(1)
    penalty = jnp.concatenate(
        [jnp.full((1, Q_BLOCK), jnp.where(i - (KEY_BLOCKS - 1) + kb >= 0, 0.0, MASKED), F32)
         for kb in range(KEY_BLOCKS)], axis=1)
    head_cols = [slice(h * HEAD_DIM, (h + 1) * HEAD_DIM) for h in range(HEADS)]
    scores = []
    for h, cols in enumerate(head_cols):
        q = (q_ref[:, cols].astype(F32) * scale).astype(BF16)
        k_band = jnp.concatenate([k_ref[:, cols] for k_ref in k_refs], axis=0)
        s = lax.dot_general(q, k_band, NT_DIMS, preferred_element_type=F32)
        scores.append(s + (bias_ref[h] + penalty))
    probs, denoms = [], []
    for s in scores:
        p = jnp.exp(s - jnp.max(s, axis=-1, keepdims=True))
        denoms.append(jnp.sum(p, axis=-1, keepdims=True))
        probs.append(p.astype(BF16))
    for h, cols in enumerate(head_cols):
        v_band = jnp.concatenate([v_ref[:, cols] for v_ref in v_refs], axis=0)
        o = jnp.dot(probs[h], v_band, preferred_element_type=F32)
        o_ref[:, cols] = (o / denoms[h]).astype(o_ref.dtype)


def _band_bias_kernel(tab_ref, o_ref):
    band = KEY_BLOCKS * Q_BLOCK
    tq = lax.broadcasted_iota(jnp.int32, (Q_BLOCK, band), 0)
    tk = lax.broadcasted_iota(jnp.int32, (Q_BLOCK, band), 1)
    q_chunk = tq // CHUNK
    k_chunk = tk // CHUNK
    in_band = (k_chunk >= q_chunk) & (k_chunk <= q_chunk + LEFT_CHUNKS)
    for h in range(HEADS):
        rows = jnp.broadcast_to(tab_ref[h:h + 1, :], (Q_BLOCK, band + Q_BLOCK))
        toeplitz = pltpu.roll(rows, 0, 1, stride=1, stride_axis=0)
        o_ref[h] = jnp.where(in_band, toeplitz[:, Q_BLOCK:], MASKED)


def _band_bias(rel_bias):
    band = KEY_BLOCKS * Q_BLOCK
    offset = jnp.arange(band + Q_BLOCK) - Q_BLOCK
    rel = (KEY_BLOCKS - 1) * Q_BLOCK - offset
    table = rel_bias.astype(F32)[:, jnp.clip(rel, -REL_CLIP, REL_CLIP) + REL_CLIP]
    return pl.pallas_call(
        _band_bias_kernel,
        out_shape=jax.ShapeDtypeStruct((HEADS, Q_BLOCK, band), F32),
        name="band_bias",
    )(table)


def _attention(proj, rel_bias, batch, seq):
    t = proj.shape[0]
    steps = seq // Q_BLOCK

    def kv_spec(kb, col):
        return pl.BlockSpec(
            (Q_BLOCK, GROUP_WIDTH),
            lambda b, i: (b * steps + jnp.maximum(i - (KEY_BLOCKS - 1) + kb, 0), col))

    bias = _band_bias(rel_bias)
    return pl.pallas_call(
        functools.partial(_attn_kernel, scale=HEAD_DIM ** -0.5),
        grid=(batch, steps),
        in_specs=([pl.BlockSpec((Q_BLOCK, GROUP_WIDTH), lambda b, i: (b * steps + i, 0))]
                  + [kv_spec(kb, 1) for kb in range(KEY_BLOCKS)]
                  + [kv_spec(kb, 2) for kb in range(KEY_BLOCKS)]
                  + [pl.BlockSpec(bias.shape, lambda b, i: (0, 0, 0))]),
        out_specs=pl.BlockSpec((Q_BLOCK, GROUP_WIDTH), lambda b, i: (b * steps + i, 0)),
        out_shape=jax.ShapeDtypeStruct((t, GROUP_WIDTH), BF16),
        compiler_params=_params("parallel", "arbitrary"),
        name="chunk_attn",
    )(proj, *([proj] * (2 * KEY_BLOCKS)), bias)


def _out_proj_kernel(yh_ref, ya_ref, w_ref, x_ref, g_ref, h_ref, a_ref):
    mix = (jnp.dot(yh_ref[...], w_ref[0:GROUP_WIDTH, :], preferred_element_type=F32)
           + jnp.dot(ya_ref[...], w_ref[GROUP_WIDTH:2 * GROUP_WIDTH, :],
                     preferred_element_type=F32))
    h = x_ref[...] + mix
    h_ref[...] = h
    a_ref[...] = (h * _rms_scale(h) * g_ref[...]).astype(BF16)


def _out_proj(y_hg, y_att, w, x, g, tm):
    t, d = x.shape
    tm = min(tm, t)
    rows = lambda width: pl.BlockSpec((tm, width), lambda i: (i, 0))
    return pl.pallas_call(
        _out_proj_kernel,
        grid=(t // tm,),
        in_specs=[rows(GROUP_WIDTH), rows(GROUP_WIDTH),
                  pl.BlockSpec(w.shape, lambda i: (0, 0)),
                  rows(d), pl.BlockSpec((1, d), lambda i: (0, 0))],
        out_specs=[rows(d), rows(d)],
        out_shape=[jax.ShapeDtypeStruct((t, d), F32),
                   jax.ShapeDtypeStruct((t, d), BF16)],
        compiler_params=_params("parallel"),
        name="out_proj",
    )(y_hg, y_att, w, x, g)


def _ffn_up_kernel(a_ref, wg_ref, wv_ref, cwg_ref, cwv_ref, cbg_ref, cbv_ref, o_ref,
                   ug0_ref, uv0_ref, ug1_ref, uv1_ref, wg_bf16_ref, wv_bf16_ref, *,
                   n_tiles, row_blocks, blocks_per_seq, tm):
    s = pl.program_id(0)
    slabs = ug0_ref.shape[0]

    @pl.when(s == 0)
    def _():
        ug1_ref[...] = jnp.zeros_like(ug1_ref)
        uv1_ref[...] = jnp.zeros_like(uv1_ref)

    row_block = jnp.minimum(s, n_tiles * row_blocks - 1) % row_blocks

    @pl.when(row_block == 0)
    def _():
        wg_bf16_ref[...] = wg_ref[...].astype(BF16)
        wv_bf16_ref[...] = wv_ref[...].astype(BF16)

    def step(new, old):
        first = (row_block % blocks_per_seq) == 0
        a = a_ref[...]
        for u_new, u_old, w_ref in ((new[0], old[0], wg_bf16_ref),
                                    (new[1], old[1], wv_bf16_ref)):
            u = jnp.dot(a, w_ref[...], preferred_element_type=F32)
            for k in range(slabs):
                tail = u_old[k, pl.ds(tm, HALO), :]
                u_new[k, 0:HALO, :] = jnp.where(first, jnp.zeros_like(tail), tail)
                u_new[k, pl.ds(HALO, tm), :] = u[:, k * LANES:(k + 1) * LANES]

        zero = jnp.zeros((EPILOGUE_ROWS, LANES), F32)
        for k in range(slabs):
            lanes = slice(k * LANES, (k + 1) * LANES)
            for r in range(0, tm, EPILOGUE_ROWS):
                def causal_conv(u_ref, cw_ref, cb_ref, out):
                    for j in range(CONV_WIDTH):
                        start = r + HALO - (CONV_WIDTH - 1) + j
                        out = out + cw_ref[j:j + 1, lanes] * u_ref[
                            k, pl.ds(start, EPILOGUE_ROWS, stride=1), :]
                    return out

                gate = causal_conv(old[0], cwg_ref, cbg_ref, cbg_ref[:, lanes] + zero)
                val = causal_conv(old[1], cwv_ref, cbv_ref, cbv_ref[:, lanes])
                act = gate * _sigmoid(gate) * val
                o_ref[r:r + EPILOGUE_ROWS, lanes] = act.astype(o_ref.dtype)
                bits = lax.bitcast_convert_type(act, jnp.uint32)
                zero = lax.bitcast_convert_type((bits >> 16) >> 16, F32)

    @pl.when(s % 2 == 0)
    def _():
        step((ug0_ref, uv0_ref), (ug1_ref, uv1_ref))

    @pl.when(s % 2 == 1)
    def _():
        step((ug1_ref, uv1_ref), (ug0_ref, uv0_ref))


def _ffn_up(a, w_up, conv_w, conv_b, seq, tm, tn):
    t, d = a.shape
    d_ff = w_up.shape[1] // 2
    tm = min(tm, seq)
    n_tiles = d_ff // tn
    row_blocks = t // tm
    last = n_tiles * row_blocks - 1
    cur = lambda s: jnp.minimum(s, last)
    prev = lambda s: jnp.maximum(s - 1, 0)
    cur_gate = lambda rows: pl.BlockSpec((rows, tn), lambda s: (0, cur(s) // row_blocks))
    cur_val = lambda rows: pl.BlockSpec((rows, tn),
                                        lambda s: (0, n_tiles + cur(s) // row_blocks))
    prev_gate = lambda rows: pl.BlockSpec((rows, tn), lambda s: (0, prev(s) // row_blocks))
    prev_val = lambda rows: pl.BlockSpec((rows, tn),
                                         lambda s: (0, n_tiles + prev(s) // row_blocks))
    u_buffer = pltpu.VMEM((tn // LANES, tm + HALO, LANES), F32)
    return pl.pallas_call(
        functools.partial(_ffn_up_kernel, n_tiles=n_tiles, row_blocks=row_blocks,
                          blocks_per_seq=seq // tm, tm=tm),
        grid=(last + 2,),
        in_specs=[pl.BlockSpec((tm, d), lambda s: (cur(s) % row_blocks, 0)),
                  cur_gate(d), cur_val(d),
                  prev_gate(CONV_WIDTH), prev_val(CONV_WIDTH),
                  prev_gate(1), prev_val(1)],
        out_specs=pl.BlockSpec((tm, tn),
                               lambda s: (prev(s) % row_blocks, prev(s) // row_blocks)),
        out_shape=jax.ShapeDtypeStruct((t, d_ff), BF16),
        scratch_shapes=[u_buffer, u_buffer, u_buffer, u_buffer,
                        pltpu.VMEM((d, tn), BF16), pltpu.VMEM((d, tn), BF16)],
        compiler_params=_params("arbitrary"),
        name="ffn_up",
    )(a, w_up, w_up, conv_w, conv_w, conv_b, conv_b)


def _ffn_down_kernel(act_ref, w_ref, h_ref, o_ref):
    o_ref[...] = h_ref[...] + jnp.dot(act_ref[...], w_ref[...],
                                      preferred_element_type=F32)


def _ffn_down(act, w, h, tm, tn):
    t, d_ff = act.shape
    d = w.shape[1]
    tm = min(tm, t)
    return pl.pallas_call(
        _ffn_down_kernel,
        grid=(t // tm, d // tn),
        in_specs=[pl.BlockSpec((tm, d_ff), lambda i, j: (i, 0)),
                  pl.BlockSpec((d_ff, tn), lambda i, j: (0, j)),
                  pl.BlockSpec((tm, tn), lambda i, j: (i, j))],
        out_specs=pl.BlockSpec((tm, tn), lambda i, j: (i, j)),
        out_shape=jax.ShapeDtypeStruct((t, d), F32),
        compiler_params=_params("parallel", "arbitrary"),
        name="ffn_down",
    )(act, w, h)


def _ple_kernel(h_ref, p_ref, g_ref, wg_ref, wp_ref, fn_ref, o_ref, *, last_layer):
    h = h_ref[...]
    a = (h * _rms_scale(h) * g_ref[...]).astype(BF16)
    gate = _sigmoid(jnp.dot(a, wg_ref[...], preferred_element_type=F32))
    emb = jnp.dot(p_ref[...].astype(BF16), wp_ref[...], preferred_element_type=F32)
    h = h + gate * emb
    if last_layer:
        h = h * _rms_scale(h) * fn_ref[...]
    o_ref[...] = h


def _ple_final(h, p, g, w_gate, w_proj, final_g, last_layer, tm):
    t, d = h.shape
    tm = min(tm, t)
    whole = lambda arr: pl.BlockSpec(arr.shape, lambda i: (0, 0))
    return pl.pallas_call(
        functools.partial(_ple_kernel, last_layer=last_layer),
        grid=(t // tm,),
        in_specs=[pl.BlockSpec((tm, d), lambda i: (i, 0)),
                  pl.BlockSpec((tm, p.shape[1]), lambda i: (i, 0)),
                  whole(g), whole(w_gate), whole(w_proj), whole(final_g)],
        out_specs=pl.BlockSpec((tm, d), lambda i: (i, 0)),
        out_shape=jax.ShapeDtypeStruct((t, d), F32),
        compiler_params=_params("parallel"),
        name="ple_final",
    )(h, p, g, w_gate, w_proj, final_g)


def kernel(x, p, norm_mix, w_in, lb_logits, hg_norm, rel_bias, w_out, norm_ffn, w_up,
           conv_w, conv_b, w_down, norm_ple, w_ple_gate, w_ple_proj, final_norm):
    batch, seq, d = x.shape
    depth = w_in.shape[0]
    t = batch * seq
    hg_cols = 4 * GROUP_WIDTH
    row = lambda vec: vec.reshape(1, -1)

    h = x.reshape(t, d)
    for layer in range(depth):
        proj_hg, proj_att = _in_proj(h, row(norm_mix[layer]), w_in[layer].astype(BF16),
                                     hg_cols, tm=1024, tn=1024)
        y_hg = _hgrn(proj_hg, lb_logits, row(hg_norm[layer]), layer, batch, seq,
                     rows_per_step=256)
        y_att = _attention(proj_att, rel_bias[layer], batch, seq)
        h, a = _out_proj(y_hg, y_att, w_out[layer].astype(BF16), h,
                         row(norm_ffn[layer]), tm=512)
        act = _ffn_up(a, w_up[layer], conv_w[layer], row(conv_b[layer]),
                      seq, tm=1024, tn=512)
        h = _ffn_down(act, w_down[layer].astype(BF16), h, tm=1024, tn=512)
        h = _ple_final(h, p[layer].reshape(t, -1), row(norm_ple[layer]),
                       w_ple_gate[layer].astype(BF16), w_ple_proj[layer].astype(BF16),
                       row(final_norm), last_layer=layer == depth - 1, tm=512)
    return h.reshape(batch, seq, d)
```

```python
import functools

import jax
import jax.numpy as jnp
from jax import lax
from jax.experimental import pallas as pl
from jax.experimental.pallas import tpu as pltpu

F32 = jnp.float32
BF16 = jnp.bfloat16

EPS = 1e-6
CHUNK = 64
HEADS = 8
HEAD_DIM = 128
GROUP_WIDTH = HEADS * HEAD_DIM
LEFT_CHUNKS = 8
REL_CLIP = 128
CONV_WIDTH = 3
Q_BLOCK = 2 * CHUNK
KEY_BLOCKS = LEFT_CHUNKS * CHUNK // Q_BLOCK + 1
MASKED = -1e30
HALO = 8
LANES = 128
SCORES_AHEAD = 4
LEVELS = 6
LOG2_E = 1.4426950408889634
EPILOGUE_ROWS = 32
VMEM_LIMIT_BYTES = 56 * 1024 * 1024

NT_DIMS = (((1,), (1,)), ((), ()))
TN_DIMS = (((0,), (0,)), ((), ()))


def _params(*semantics, flags=None):
    return pltpu.CompilerParams(dimension_semantics=semantics,
                                vmem_limit_bytes=VMEM_LIMIT_BYTES, flags=flags)


def _rms_scale(x):
    return lax.rsqrt(jnp.mean(x * x, axis=-1, keepdims=True) + EPS)


def _sigmoid(x):
    return 0.5 * jnp.tanh(0.5 * x) + 0.5


def _in_proj_kernel(x_ref, g_ref, w_ref, hg_ref, att_ref, a_ref, *, hg_tiles):
    j = pl.program_id(1)

    @pl.when(j == 0)
    def _():
        x = x_ref[...]
        a_ref[...] = (x * _rms_scale(x) * g_ref[...]).astype(BF16)

    @pl.when(j < hg_tiles)
    def _():
        hg_ref[...] = jnp.dot(a_ref[...], w_ref[...], preferred_element_type=F32)

    @pl.when(j >= hg_tiles)
    def _():
        att_ref[...] = jnp.dot(a_ref[...], w_ref[...],
                               preferred_element_type=F32).astype(att_ref.dtype)


def _in_proj(x, g, w, hg_cols, tm, tn):
    t, d = x.shape
    n = w.shape[1]
    tm = min(tm, t)
    hg_tiles = hg_cols // tn
    att_tiles = (n - hg_cols) // tn
    assert tn == GROUP_WIDTH and att_tiles == 3
    att_tile = lambda j: (jnp.maximum(j - hg_tiles, 0) + att_tiles - 1) % att_tiles
    return pl.pallas_call(
        functools.partial(_in_proj_kernel, hg_tiles=hg_tiles),
        grid=(t // tm, n // tn),
        in_specs=[pl.BlockSpec((tm, d), lambda i, j: (i, 0)),
                  pl.BlockSpec((1, d), lambda i, j: (0, 0)),
                  pl.BlockSpec((d, tn), lambda i, j: (0, j))],
        out_specs=[pl.BlockSpec((tm, tn), lambda i, j: (i, jnp.minimum(j, hg_tiles - 1))),
                   pl.BlockSpec((tm, tn), lambda i, j: (i, att_tile(j)))],
        out_shape=[jax.ShapeDtypeStruct((t, hg_cols), F32),
                   jax.ShapeDtypeStruct((t, n - hg_cols), BF16)],
        scratch_shapes=[pltpu.VMEM((tm, d), BF16)],
        compiler_params=_params("parallel", "arbitrary"),
        name="in_proj",
    )(x, g, w)


def _select_rows(level, on_set, on_clear, bit_masks):
    n = 1 << level
    if n >= 8:
        parts = [(on_set if (r >> level) & 1 else on_clear)[r:r + 8]
                 for r in range(0, CHUNK, 8)]
        return jnp.concatenate(parts, axis=0)
    return jnp.where(bit_masks[level], on_set, on_clear)


def _swap_row_blocks(level, x, bit_masks):
    n = 1 << level
    if n >= 8:
        parts = [x[r ^ n:(r ^ n) + 8] for r in range(0, CHUNK, 8)]
        return jnp.concatenate(parts, axis=0)
    tiles = x.reshape(CHUNK // 8, 8, x.shape[-1])
    down = pltpu.roll(tiles, n, 1).reshape(x.shape)
    if 2 * n == 8:
        return down
    up = pltpu.roll(tiles, 8 - n, 1).reshape(x.shape)
    return jnp.where(bit_masks[level], down, up)


def _hgrn_level_products(q, f_pre, lb, bit_masks):
    sig = _sigmoid(f_pre)
    log_f = jnp.log(lb + (1.0 - lb) * sig) * LOG2_E
    k = (1.0 - lb) * (1.0 - sig)
    qf = q * _sigmoid(q)
    zeros = jnp.zeros_like(log_f)

    seg = log_f
    tot = log_f
    products = []
    for level in range(LEVELS):
        expo = _select_rows(level, seg, tot - seg, bit_masks)
        x = (jnp.exp2(expo) * _select_rows(level, qf, k, bit_masks)).astype(BF16)
        products.append(lax.dot_general(x, x, NT_DIMS, preferred_element_type=F32))
        sibling = _swap_row_blocks(level, tot, bit_masks)
        seg = seg + _select_rows(level, sibling, zeros, bit_masks)
        tot = tot + sibling
    return qf, k, products, seg, tot


def _hgrn_scores(products, pair_level):
    tiles = []
    for r in range(CHUNK // 8):
        rows = slice(8 * r, 8 * r + 8)
        level_of = pair_level[rows]
        acc = jnp.zeros((8, CHUNK), F32)
        for level in range(LEVELS):
            if level >= 3 and not (r >> (level - 3)) & 1:
                continue
            acc = jnp.where(level_of == level, products[level][rows], acc)
        tiles.append(acc)
    return jnp.concatenate(tiles, axis=0)


def _hgrn_kernel(q_ref, f_ref, i_ref, g_ref, lbl_ref, ng_ref, y_ref, st_ref, *,
                 layer, n_chunks):
    @pl.when(pl.program_id(1) == 0)
    def _():
        st_ref[...] = jnp.zeros_like(st_ref)

    lbl = lbl_ref[...]
    e = jnp.exp(lbl - jnp.max(lbl, axis=0, keepdims=True))
    lb_all = (jnp.sum(e[0:layer + 1, :], axis=0, keepdims=True)
              / jnp.sum(e, axis=0, keepdims=True))
    norm_g = ng_ref[...]

    rows = lax.broadcasted_iota(jnp.int32, (CHUNK, HEAD_DIM), 0)
    bit_masks = [((rows >> level) & 1) == 1 for level in range(3)]
    t = lax.broadcasted_iota(jnp.int32, (CHUNK, CHUNK), 0)
    s = lax.broadcasted_iota(jnp.int32, (CHUNK, CHUNK), 1)
    pair_level = jnp.where(t > s, 31 - lax.clz(t ^ s), -1)
    head_cols = [slice(h * HEAD_DIM, (h + 1) * HEAD_DIM) for h in range(HEADS)]

    def chunk_body(c, carry):
        r0 = pl.multiple_of(c * CHUNK, CHUNK)
        chunk_rows = pl.ds(r0, CHUNK)
        phase1 = [_hgrn_level_products(q_ref[chunk_rows, cols], f_ref[chunk_rows, cols],
                                       lb_all[:, cols], bit_masks) for cols in head_cols]
        outs = []
        for h, cols in enumerate(head_cols):
            qf, k, products, b, b_last = phase1[h]
            v = i_ref[chunk_rows, cols]
            vb = v.astype(BF16)
            state_t = st_ref[h]
            scores = _hgrn_scores(products, pair_level)
            diag = jnp.sum(qf * k, axis=-1, keepdims=True)
            o = jnp.dot(scores.astype(BF16), vb, preferred_element_type=F32) + diag * v
            qb = (qf * jnp.exp2(b)).astype(BF16)
            o = o + lax.dot_general(qb, state_t.astype(BF16), NT_DIMS,
                                    preferred_element_type=F32)
            k_end = (k * jnp.exp2(b_last - b)).astype(BF16)
            st_ref[h] = state_t * jnp.exp2(b_last[0:1, :]) + lax.dot_general(
                vb, k_end, TN_DIMS, preferred_element_type=F32)
            outs.append(o)
        for h, cols in enumerate(head_cols):
            o = outs[h]
            g = g_ref[chunk_rows, cols]
            y = o * _rms_scale(o) * norm_g * (g * _sigmoid(g))
            y_ref[chunk_rows, cols] = y.astype(y_ref.dtype)
        return carry

    lax.fori_loop(0, n_chunks, chunk_body, 0)


def _hgrn(proj, lb_logits, norm_g, layer, batch, seq, rows_per_step):
    t = proj.shape[0]
    tb = min(rows_per_step, seq)
    steps = seq // tb
    blk = lambda col: pl.BlockSpec((tb, GROUP_WIDTH),
                                   lambda b, s: (b * steps + s, col))
    return pl.pallas_call(
        functools.partial(_hgrn_kernel, layer=layer, n_chunks=tb // CHUNK),
        grid=(batch, steps),
        in_specs=[blk(0), blk(1), blk(2), blk(3),
                  pl.BlockSpec(lb_logits.shape, lambda b, s: (0, 0)),
                  pl.BlockSpec((1, HEAD_DIM), lambda b, s: (0, 0))],
        out_specs=pl.BlockSpec((tb, GROUP_WIDTH), lambda b, s: (b * steps + s, 0)),
        out_shape=jax.ShapeDtypeStruct((t, GROUP_WIDTH), BF16),
        scratch_shapes=[pltpu.VMEM((HEADS, HEAD_DIM, HEAD_DIM), F32)],
        compiler_params=_params("arbitrary", "arbitrary"),
        name="hgrn2",
    )(proj, proj, proj, proj, lb_logits, norm_g)


def _attn_kernel(q_ref, *refs, scale):
    kv_refs = refs[:KEY_BLOCKS]
    bias_ref, o_ref = refs[KEY_BLOCKS:]
    i = pl.program_id(1)
    head_cols = [slice(h * HEAD_DIM, (h + 1) * HEAD_DIM) for h in range(HEADS)]

    def attend(penalty):
        def scores_of(h):
            cols = head_cols[h]
            q = (q_ref[:, cols].astype(F32) * (scale * LOG2_E)).astype(BF16)
            k_band = jnp.concatenate([kv_ref[:, cols] for kv_ref in kv_refs], axis=0)
            s = lax.dot_general(q, k_band, NT_DIMS, preferred_element_type=F32)
            s = s + bias_ref[h]
            return s if penalty is None else s + penalty

        def finish(h, s):
            cols = head_cols[h]
            p = jnp.exp2(s - jnp.max(s, axis=-1, keepdims=True))
            denom = jnp.sum(p, axis=-1, keepdims=True)
            v_cols = slice(GROUP_WIDTH + cols.start, GROUP_WIDTH + cols.stop)
            v_band = jnp.concatenate([kv_ref[:, v_cols] for kv_ref in kv_refs], axis=0)
            o = jnp.dot(p.astype(BF16), v_band, preferred_element_type=F32)
            o_ref[:, cols] = (o / denom).astype(o_ref.dtype)

        pending = [scores_of(h) for h in range(SCORES_AHEAD)]
        for h in range(HEADS):
            if h + SCORES_AHEAD < HEADS:
                pending.append(scores_of(h + SCORES_AHEAD))
            finish(h, pending[h])

    @pl.when(i >= KEY_BLOCKS - 1)
    def _():
        attend(None)

    @pl.when(i < KEY_BLOCKS - 1)
    def _():
        attend(jnp.concatenate(
            [jnp.full((1, Q_BLOCK), jnp.where(i - (KEY_BLOCKS - 1) + kb >= 0, 0.0, MASKED), F32)
             for kb in range(KEY_BLOCKS)], axis=1))


def _band_bias_kernel(tab_ref, o_ref):
    band = KEY_BLOCKS * Q_BLOCK
    tq = lax.broadcasted_iota(jnp.int32, (Q_BLOCK, band), 0)
    tk = lax.broadcasted_iota(jnp.int32, (Q_BLOCK, band), 1)
    q_chunk = tq // CHUNK
    k_chunk = tk // CHUNK
    in_band = (k_chunk >= q_chunk) & (k_chunk <= q_chunk + LEFT_CHUNKS)
    for h in range(HEADS):
        rows = jnp.broadcast_to(tab_ref[h:h + 1, :], (Q_BLOCK, band + Q_BLOCK))
        toeplitz = pltpu.roll(rows, 0, 1, stride=1, stride_axis=0)
        o_ref[h] = jnp.where(in_band, toeplitz[:, Q_BLOCK:] * LOG2_E, MASKED)


def _band_bias(rel_bias):
    band = KEY_BLOCKS * Q_BLOCK
    offset = jnp.arange(band + Q_BLOCK) - Q_BLOCK
    rel = (KEY_BLOCKS - 1) * Q_BLOCK - offset
    table = rel_bias.astype(F32)[:, jnp.clip(rel, -REL_CLIP, REL_CLIP) + REL_CLIP]
    return pl.pallas_call(
        _band_bias_kernel,
        out_shape=jax.ShapeDtypeStruct((HEADS, Q_BLOCK, band), F32),
        name="band_bias",
    )(table)


def _attention(proj, rel_bias, batch, seq):
    t = proj.shape[0]
    steps = seq // Q_BLOCK

    def kv_spec(kb):
        return pl.BlockSpec(
            (Q_BLOCK, 2 * GROUP_WIDTH),
            lambda b, i: (b * steps + jnp.maximum(i - (KEY_BLOCKS - 1) + kb, 0), 0))

    bias = _band_bias(rel_bias)
    return pl.pallas_call(
        functools.partial(_attn_kernel, scale=HEAD_DIM ** -0.5),
        grid=(batch, steps),
        in_specs=([pl.BlockSpec((Q_BLOCK, GROUP_WIDTH), lambda b, i: (b * steps + i, 2))]
                  + [kv_spec(kb) for kb in range(KEY_BLOCKS)]
                  + [pl.BlockSpec(bias.shape, lambda b, i: (0, 0, 0))]),
        out_specs=pl.BlockSpec((Q_BLOCK, GROUP_WIDTH), lambda b, i: (b * steps + i, 0)),
        out_shape=jax.ShapeDtypeStruct((t, GROUP_WIDTH), BF16),
        compiler_params=_params("parallel", "arbitrary"),
        name="chunk_attn",
    )(proj, *([proj] * KEY_BLOCKS), bias)


def _out_proj_kernel(yh_ref, ya_ref, w_ref, x_ref, g_ref, h_ref, a_ref):
    mix = (jnp.dot(yh_ref[...], w_ref[0:GROUP_WIDTH, :], preferred_element_type=F32)
           + jnp.dot(ya_ref[...], w_ref[GROUP_WIDTH:2 * GROUP_WIDTH, :],
                     preferred_element_type=F32))
    h = x_ref[...] + mix
    h_ref[...] = h
    a_ref[...] = (h * _rms_scale(h) * g_ref[...]).astype(BF16)


def _out_proj(y_hg, y_att, w, x, g, tm):
    t, d = x.shape
    tm = min(tm, t)
    rows = lambda width: pl.BlockSpec((tm, width), lambda i: (i, 0))
    return pl.pallas_call(
        _out_proj_kernel,
        grid=(t // tm,),
        in_specs=[rows(GROUP_WIDTH), rows(GROUP_WIDTH),
                  pl.BlockSpec(w.shape, lambda i: (0, 0)),
                  rows(d), pl.BlockSpec((1, d), lambda i: (0, 0))],
        out_specs=[rows(d), rows(d)],
        out_shape=[jax.ShapeDtypeStruct((t, d), F32),
                   jax.ShapeDtypeStruct((t, d), BF16)],
        compiler_params=_params("parallel"),
        name="out_proj",
    )(y_hg, y_att, w, x, g)


def _ffn_up_kernel(a_ref, wg_ref, wv_ref, cwg_ref, cwv_ref, cbg_ref, cbv_ref, o_ref,
                   ug0_ref, uv0_ref, ug1_ref, uv1_ref, wg_bf16_ref, wv_bf16_ref, *,
                   n_tiles, row_blocks, blocks_per_seq, tm):
    s = pl.program_id(0)
    slabs = ug0_ref.shape[0]

    @pl.when(s == 0)
    def _():
        ug1_ref[...] = jnp.zeros_like(ug1_ref)
        uv1_ref[...] = jnp.zeros_like(uv1_ref)

    row_block = jnp.minimum(s, n_tiles * row_blocks - 1) % row_blocks

    @pl.when(row_block == 0)
    def _():
        wg_bf16_ref[...] = wg_ref[...].astype(BF16)
        wv_bf16_ref[...] = wv_ref[...].astype(BF16)

    def step(new, old):
        first = (row_block % blocks_per_seq) == 0
        a = a_ref[...]
        for u_new, u_old, w_ref in ((new[0], old[0], wg_bf16_ref),
                                    (new[1], old[1], wv_bf16_ref)):
            u = jnp.dot(a, w_ref[...], preferred_element_type=F32)
            for k in range(slabs):
                tail = u_old[k, pl.ds(tm, HALO), :]
                u_new[k, 0:HALO, :] = jnp.where(first, jnp.zeros_like(tail), tail)
                u_new[k, pl.ds(HALO, tm), :] = u[:, k * LANES:(k + 1) * LANES]

        zero = jnp.zeros((EPILOGUE_ROWS, LANES), F32)
        for k in range(slabs):
            lanes = slice(k * LANES, (k + 1) * LANES)
            for r in range(0, tm, EPILOGUE_ROWS):
                def causal_conv(u_ref, cw_ref, cb_ref, out):
                    for j in range(CONV_WIDTH):
                        start = r + HALO - (CONV_WIDTH - 1) + j
                        out = out + cw_ref[j:j + 1, lanes] * u_ref[
                            k, pl.ds(start, EPILOGUE_ROWS, stride=1), :]
                    return out

                gate = causal_conv(old[0], cwg_ref, cbg_ref, cbg_ref[:, lanes] + zero)
                val = causal_conv(old[1], cwv_ref, cbv_ref, cbv_ref[:, lanes])
                act = gate * _sigmoid(gate) * val
                o_ref[r:r + EPILOGUE_ROWS, lanes] = act.astype(o_ref.dtype)
                bits = lax.bitcast_convert_type(act, jnp.uint32)
                zero = lax.bitcast_convert_type((bits >> 16) >> 16, F32)

    @pl.when(s % 2 == 0)
    def _():
        step((ug0_ref, uv0_ref), (ug1_ref, uv1_ref))

    @pl.when(s % 2 == 1)
    def _():
        step((ug1_ref, uv1_ref), (ug0_ref, uv0_ref))


def _ffn_up(a, w_up, conv_w, conv_b, seq, tm, tn):
    t, d = a.shape
    d_ff = w_up.shape[1] // 2
    tm = min(tm, seq)
    n_tiles = d_ff // tn
    row_blocks = t // tm
    last = n_tiles * row_blocks - 1
    cur = lambda s: jnp.minimum(s, last)
    prev = lambda s: jnp.maximum(s - 1, 0)
    cur_gate = lambda rows: pl.BlockSpec((rows, tn), lambda s: (0, cur(s) // row_blocks))
    cur_val = lambda rows: pl.BlockSpec((rows, tn),
                                        lambda s: (0, n_tiles + cur(s) // row_blocks))
    prev_gate = lambda rows: pl.BlockSpec((rows, tn), lambda s: (0, prev(s) // row_blocks))
    prev_val = lambda rows: pl.BlockSpec((rows, tn),
                                         lambda s: (0, n_tiles + prev(s) // row_blocks))
    u_buffer = pltpu.VMEM((tn // LANES, tm + HALO, LANES), F32)
    return pl.pallas_call(
        functools.partial(_ffn_up_kernel, n_tiles=n_tiles, row_blocks=row_blocks,
                          blocks_per_seq=seq // tm, tm=tm),
        grid=(last + 2,),
        in_specs=[pl.BlockSpec((tm, d), lambda s: (cur(s) % row_blocks, 0)),
                  cur_gate(d), cur_val(d),
                  prev_gate(CONV_WIDTH), prev_val(CONV_WIDTH),
                  prev_gate(1), prev_val(1)],
        out_specs=pl.BlockSpec((tm, tn),
                               lambda s: (prev(s) % row_blocks, prev(s) // row_blocks)),
        out_shape=jax.ShapeDtypeStruct((t, d_ff), BF16),
        scratch_shapes=[u_buffer, u_buffer, u_buffer, u_buffer,
                        pltpu.VMEM((d, tn), BF16), pltpu.VMEM((d, tn), BF16)],
        compiler_params=_params("arbitrary"),
        name="ffn_up",
    )(a, w_up, w_up, conv_w, conv_w, conv_b, conv_b)


def _ffn_down_kernel(act_ref, w_ref, h_ref, o_ref):
    o_ref[...] = h_ref[...] + jnp.dot(act_ref[...], w_ref[...],
                                      preferred_element_type=F32)


def _ffn_down(act, w, h, tm, tn):
    t, d_ff = act.shape
    d = w.shape[1]
    tm = min(tm, t)
    return pl.pallas_call(
        _ffn_down_kernel,
        grid=(t // tm, d // tn),
        in_specs=[pl.BlockSpec((tm, d_ff), lambda i, j: (i, 0)),
                  pl.BlockSpec((d_ff, tn), lambda i, j: (0, j)),
                  pl.BlockSpec((tm, tn), lambda i, j: (i, j))],
        out_specs=pl.BlockSpec((tm, tn), lambda i, j: (i, j)),
        out_shape=jax.ShapeDtypeStruct((t, d), F32),
        compiler_params=_params("parallel", "arbitrary"),
        name="ffn_down",
    )(act, w, h)


def _ple_kernel(h_ref, p_ref, g_ref, wg_ref, wp_ref, fn_ref, o_ref, *, last_layer):
    h = h_ref[...]
    a = (h * _rms_scale(h) * g_ref[...]).astype(BF16)
    gate = _sigmoid(jnp.dot(a, wg_ref[...], preferred_element_type=F32))
    emb = jnp.dot(p_ref[...].astype(BF16), wp_ref[...], preferred_element_type=F32)
    h = h + gate * emb
    if last_layer:
        h = h * _rms_scale(h) * fn_ref[...]
    o_ref[...] = h


def _ple_final(h, p, g, w_gate, w_proj, final_g, last_layer, tm):
    t, d = h.shape
    tm = min(tm, t)
    whole = lambda arr: pl.BlockSpec(arr.shape, lambda i: (0, 0))
    return pl.pallas_call(
        functools.partial(_ple_kernel, last_layer=last_layer),
        grid=(t // tm,),
        in_specs=[pl.BlockSpec((tm, d), lambda i: (i, 0)),
                  pl.BlockSpec((tm, p.shape[1]), lambda i: (i, 0)),
                  whole(g), whole(w_gate), whole(w_proj), whole(final_g)],
        out_specs=pl.BlockSpec((tm, d), lambda i: (i, 0)),
        out_shape=jax.ShapeDtypeStruct((t, d), F32),
        compiler_params=_params("parallel"),
        name="ple_final",
    )(h, p, g, w_gate, w_proj, final_g)


def kernel(x, p, norm_mix, w_in, lb_logits, hg_norm, rel_bias, w_out, norm_ffn, w_up,
           conv_w, conv_b, w_down, norm_ple, w_ple_gate, w_ple_proj, final_norm):
    batch, seq, d = x.shape
    depth = w_in.shape[0]
    t = batch * seq
    hg_cols = 4 * GROUP_WIDTH
    row = lambda vec: vec.reshape(1, -1)

    h = x.reshape(t, d)
    for layer in range(depth):
        proj_hg, proj_att = _in_proj(h, row(norm_mix[layer]), w_in[layer].astype(BF16),
                                     hg_cols, tm=1024, tn=1024)
        y_hg = _hgrn(proj_hg, lb_logits, row(hg_norm[layer]), layer, batch, seq,
                     rows_per_step=256)
        y_att = _attention(proj_att, rel_bias[layer], batch, seq)
        h, a = _out_proj(y_hg, y_att, w_out[layer].astype(BF16), h,
                         row(norm_ffn[layer]), tm=512)
        act = _ffn_up(a, w_up[layer], conv_w[layer], row(conv_b[layer]),
                      seq, tm=1024, tn=512)
        h = _ffn_down(act, w_down[layer].astype(BF16), h, tm=1024, tn=512)
        h = _ple_final(h, p[layer].reshape(t, -1), row(norm_ple[layer]),
                       w_ple_gate[layer].astype(BF16), w_ple_proj[layer].astype(BF16),
                       row(final_norm), last_layer=layer == depth - 1, tm=512)
    return h.reshape(batch, seq, d)
```

```python
import functools

import jax
import jax.numpy as jnp
from jax import lax
from jax.experimental import pallas as pl
from jax.experimental.pallas import tpu as pltpu

F32 = jnp.float32
BF16 = jnp.bfloat16

EPS = 1e-6
CHUNK = 64
HEADS = 8
HEAD_DIM = 128
GROUP_WIDTH = HEADS * HEAD_DIM
LEFT_CHUNKS = 8
REL_CLIP = 128
CONV_WIDTH = 3
Q_BLOCK = 2 * CHUNK
KEY_BLOCKS = LEFT_CHUNKS * CHUNK // Q_BLOCK + 1
MASKED = -1e30
HALO = 8
LANES = 128
SCORES_AHEAD = 4
LEVELS = 6
LOG2_E = 1.4426950408889634
EPILOGUE_ROWS = 32
VMEM_LIMIT_BYTES = 56 * 1024 * 1024

NT_DIMS = (((1,), (1,)), ((), ()))
TN_DIMS = (((0,), (0,)), ((), ()))


def _params(*semantics, flags=None):
    return pltpu.CompilerParams(dimension_semantics=semantics,
                                vmem_limit_bytes=VMEM_LIMIT_BYTES, flags=flags)


def _rms_scale(x):
    return lax.rsqrt(jnp.mean(x * x, axis=-1, keepdims=True) + EPS)


def _sigmoid(x):
    return 0.5 * jnp.tanh(0.5 * x) + 0.5


def _in_proj_kernel(x_ref, g_ref, w_ref, hg_ref, att_ref, a_ref, *, hg_tiles):
    j = pl.program_id(1)

    @pl.when(j == 0)
    def _():
        x = x_ref[...]
        a_ref[...] = (x * _rms_scale(x) * g_ref[...]).astype(BF16)

    @pl.when(j < hg_tiles)
    def _():
        hg_ref[...] = jnp.dot(a_ref[...], w_ref[...], preferred_element_type=F32)

    @pl.when(j >= hg_tiles)
    def _():
        att_ref[...] = jnp.dot(a_ref[...], w_ref[...],
                               preferred_element_type=F32).astype(att_ref.dtype)


def _in_proj(x, g, w, hg_cols, tm, tn):
    t, d = x.shape
    n = w.shape[1]
    tm = min(tm, t)
    hg_tiles = hg_cols // tn
    att_tiles = (n - hg_cols) // tn
    assert tn == GROUP_WIDTH and att_tiles == 3
    att_tile = lambda j: (jnp.maximum(j - hg_tiles, 0) + att_tiles - 1) % att_tiles
    return pl.pallas_call(
        functools.partial(_in_proj_kernel, hg_tiles=hg_tiles),
        grid=(t // tm, n // tn),
        in_specs=[pl.BlockSpec((tm, d), lambda i, j: (i, 0)),
                  pl.BlockSpec((1, d), lambda i, j: (0, 0)),
                  pl.BlockSpec((d, tn), lambda i, j: (0, j))],
        out_specs=[pl.BlockSpec((tm, tn), lambda i, j: (i, jnp.minimum(j, hg_tiles - 1))),
                   pl.BlockSpec((tm, tn), lambda i, j: (i, att_tile(j)))],
        out_shape=[jax.ShapeDtypeStruct((t, hg_cols), F32),
                   jax.ShapeDtypeStruct((t, n - hg_cols), BF16)],
        scratch_shapes=[pltpu.VMEM((tm, d), BF16)],
        compiler_params=_params("parallel", "arbitrary"),
        name="in_proj",
    )(x, g, w)


def _select_rows(level, on_set, on_clear, bit_masks):
    n = 1 << level
    if n >= 8:
        parts = [(on_set if (r >> level) & 1 else on_clear)[r:r + 8]
                 for r in range(0, CHUNK, 8)]
        return jnp.concatenate(parts, axis=0)
    return jnp.where(bit_masks[level], on_set, on_clear)


def _swap_row_blocks(level, x, bit_masks):
    n = 1 << level
    if n >= 8:
        parts = [x[r ^ n:(r ^ n) + 8] for r in range(0, CHUNK, 8)]
        return jnp.concatenate(parts, axis=0)
    tiles = x.reshape(CHUNK // 8, 8, x.shape[-1])
    down = pltpu.roll(tiles, n, 1).reshape(x.shape)
    if 2 * n == 8:
        return down
    up = pltpu.roll(tiles, 8 - n, 1).reshape(x.shape)
    return jnp.where(bit_masks[level], down, up)


def _hgrn_level_products(q, f_pre, lb, bit_masks):
    sig = _sigmoid(f_pre)
    log_f = jnp.log(lb + (1.0 - lb) * sig) * LOG2_E
    k = (1.0 - lb) * (1.0 - sig)
    qf = q * _sigmoid(q)
    zeros = jnp.zeros_like(log_f)

    seg = log_f
    tot = log_f
    products = []
    for level in range(LEVELS):
        expo = _select_rows(level, seg, tot - seg, bit_masks)
        x = (jnp.exp2(expo) * _select_rows(level, qf, k, bit_masks)).astype(BF16)
        products.append(lax.dot_general(x, x, NT_DIMS, preferred_element_type=F32))
        sibling = _swap_row_blocks(level, tot, bit_masks)
        seg = seg + _select_rows(level, sibling, zeros, bit_masks)
        tot = tot + sibling
    return qf, k, products, seg, tot


def _hgrn_scores(products, pair_level):
    tiles = []
    for r in range(CHUNK // 8):
        rows = slice(8 * r, 8 * r + 8)
        level_of = pair_level[rows]
        acc = jnp.zeros((8, CHUNK), F32)
        for level in range(LEVELS):
            if level >= 3 and not (r >> (level - 3)) & 1:
                continue
            acc = jnp.where(level_of == level, products[level][rows], acc)
        tiles.append(acc)
    return jnp.concatenate(tiles, axis=0)


def _hgrn_chunk(q_ref, f_ref, i_ref, g_ref, y_ref, st_ref, rows, lb_all, norm_g, bit_masks,
                pair_level):
    head_cols = [slice(h * HEAD_DIM, (h + 1) * HEAD_DIM) for h in range(HEADS)]
    phase1 = [_hgrn_level_products(q_ref[rows, cols], f_ref[rows, cols],
                                   lb_all[:, cols], bit_masks) for cols in head_cols]
    outs = []
    for h, cols in enumerate(head_cols):
        qf, k, products, b, b_last = phase1[h]
        v = i_ref[rows, cols]
        vb = v.astype(BF16)
        state_t = st_ref[h]
        scores = _hgrn_scores(products, pair_level)
        diag = jnp.sum(qf * k, axis=-1, keepdims=True)
        o = jnp.dot(scores.astype(BF16), vb, preferred_element_type=F32) + diag * v
        qb = (qf * jnp.exp2(b)).astype(BF16)
        o = o + lax.dot_general(qb, state_t.astype(BF16), NT_DIMS,
                                preferred_element_type=F32)
        k_end = (k * jnp.exp2(b_last - b)).astype(BF16)
        st_ref[h] = state_t * jnp.exp2(b_last[0:1, :]) + lax.dot_general(
            vb, k_end, TN_DIMS, preferred_element_type=F32)
        outs.append(o)
    for h, cols in enumerate(head_cols):
        o = outs[h]
        g = g_ref[rows, cols]
        y = o * _rms_scale(o) * norm_g * (g * _sigmoid(g))
        y_ref[rows, cols] = y.astype(y_ref.dtype)


def _mixer_out_kernel(q_ref, f_ref, i_ref, g_ref, lbl_ref, ng_ref, ya_ref, x_ref, w_ref,
                      gn_ref, h_ref, a_ref, st_ref, yh0_ref, yh1_ref, *,
                      layer, steps_per_seq, n_steps, chunks):
    step_id = pl.program_id(0)
    cur = jnp.minimum(step_id, n_steps - 1)

    @pl.when(cur % steps_per_seq == 0)
    def _():
        st_ref[...] = jnp.zeros_like(st_ref)

    @pl.when(step_id == 0)
    def _():
        yh1_ref[...] = jnp.zeros_like(yh1_ref)

    lbl = lbl_ref[...]
    e = jnp.exp(lbl - jnp.max(lbl, axis=0, keepdims=True))
    lb_all = (jnp.sum(e[0:layer + 1, :], axis=0, keepdims=True)
              / jnp.sum(e, axis=0, keepdims=True))
    norm_g = ng_ref[...]

    rows = lax.broadcasted_iota(jnp.int32, (CHUNK, HEAD_DIM), 0)
    bit_masks = [((rows >> level) & 1) == 1 for level in range(3)]
    t = lax.broadcasted_iota(jnp.int32, (CHUNK, CHUNK), 0)
    s = lax.broadcasted_iota(jnp.int32, (CHUNK, CHUNK), 1)
    pair_level = jnp.where(t > s, 31 - lax.clz(t ^ s), -1)

    d = x_ref.shape[1]
    piece = d // chunks

    def step(yh_new, yh_old):
        sum_sq = jnp.zeros((x_ref.shape[0], 1), F32)
        for c in range(chunks):
            _hgrn_chunk(q_ref, f_ref, i_ref, g_ref, yh_new, st_ref,
                        slice(c * CHUNK, (c + 1) * CHUNK), lb_all, norm_g, bit_masks,
                        pair_level)
            cols = slice(c * piece, (c + 1) * piece)
            mix = (jnp.dot(yh_old[...], w_ref[0:GROUP_WIDTH, cols],
                           preferred_element_type=F32)
                   + jnp.dot(ya_ref[...], w_ref[GROUP_WIDTH:2 * GROUP_WIDTH, cols],
                             preferred_element_type=F32))
            h = x_ref[:, cols] + mix
            h_ref[:, cols] = h
            sum_sq = sum_sq + jnp.sum(h * h, axis=-1, keepdims=True)
        scale = lax.rsqrt(sum_sq * (1.0 / d) + EPS)
        a_ref[...] = (h_ref[...] * scale * gn_ref[...]).astype(a_ref.dtype)

    @pl.when(step_id % 2 == 0)
    def _():
        step(yh0_ref, yh1_ref)

    @pl.when(step_id % 2 == 1)
    def _():
        step(yh1_ref, yh0_ref)


def _mixer_out(proj, y_att, lb_logits, norm_g, w_out, x, g_ffn, layer, batch, seq,
               rows_per_step):
    t, d = x.shape
    tb = min(rows_per_step, seq)
    steps_per_seq = seq // tb
    n_steps = batch * steps_per_seq
    cur = lambda s: jnp.minimum(s, n_steps - 1)
    prev = lambda s: jnp.maximum(s - 1, 0)
    proj_blk = lambda col: pl.BlockSpec((tb, GROUP_WIDTH), lambda s: (cur(s), col))
    prev_rows = lambda width: pl.BlockSpec((tb, width), lambda s: (prev(s), 0))
    whole = lambda arr: pl.BlockSpec(arr.shape, lambda s: (0, 0))
    return pl.pallas_call(
        functools.partial(_mixer_out_kernel, layer=layer, steps_per_seq=steps_per_seq,
                          n_steps=n_steps, chunks=tb // CHUNK),
        grid=(n_steps + 1,),
        in_specs=[proj_blk(0), proj_blk(1), proj_blk(2), proj_blk(3),
                  whole(lb_logits), whole(norm_g),
                  prev_rows(GROUP_WIDTH), prev_rows(d), whole(w_out), whole(g_ffn)],
        out_specs=[prev_rows(d), prev_rows(d)],
        out_shape=[jax.ShapeDtypeStruct((t, d), F32), jax.ShapeDtypeStruct((t, d), BF16)],
        scratch_shapes=[pltpu.VMEM((HEADS, HEAD_DIM, HEAD_DIM), F32),
                        pltpu.VMEM((tb, GROUP_WIDTH), BF16),
                        pltpu.VMEM((tb, GROUP_WIDTH), BF16)],
        compiler_params=_params("arbitrary"),
        name="hgrn2_out_proj",
    )(proj, proj, proj, proj, lb_logits, norm_g, y_att, x, w_out, g_ffn)


def _attn_kernel(q_ref, *refs, scale):
    kv_refs = refs[:KEY_BLOCKS]
    bias_ref, o_ref = refs[KEY_BLOCKS:]
    i = pl.program_id(1)
    head_cols = [slice(h * HEAD_DIM, (h + 1) * HEAD_DIM) for h in range(HEADS)]

    def attend(penalty):
        def scores_of(h):
            cols = head_cols[h]
            q = (q_ref[:, cols].astype(F32) * (scale * LOG2_E)).astype(BF16)
            k_band = jnp.concatenate([kv_ref[:, cols] for kv_ref in kv_refs], axis=0)
            s = lax.dot_general(q, k_band, NT_DIMS, preferred_element_type=F32)
            s = s + bias_ref[h]
            return s if penalty is None else s + penalty

        def finish(h, s):
            cols = head_cols[h]
            p = jnp.exp2(s - jnp.max(s, axis=-1, keepdims=True))
            denom = jnp.sum(p, axis=-1, keepdims=True)
            v_cols = slice(GROUP_WIDTH + cols.start, GROUP_WIDTH + cols.stop)
            v_band = jnp.concatenate([kv_ref[:, v_cols] for kv_ref in kv_refs], axis=0)
            o = jnp.dot(p.astype(BF16), v_band, preferred_element_type=F32)
            o_ref[:, cols] = (o / denom).astype(o_ref.dtype)

        pending = [scores_of(h) for h in range(SCORES_AHEAD)]
        for h in range(HEADS):
            if h + SCORES_AHEAD < HEADS:
                pending.append(scores_of(h + SCORES_AHEAD))
            finish(h, pending[h])

    @pl.when(i >= KEY_BLOCKS - 1)
    def _():
        attend(None)

    @pl.when(i < KEY_BLOCKS - 1)
    def _():
        attend(jnp.concatenate(
            [jnp.full((1, Q_BLOCK), jnp.where(i - (KEY_BLOCKS - 1) + kb >= 0, 0.0, MASKED), F32)
             for kb in range(KEY_BLOCKS)], axis=1))


def _band_bias_kernel(tab_ref, o_ref):
    band = KEY_BLOCKS * Q_BLOCK
    tq = lax.broadcasted_iota(jnp.int32, (Q_BLOCK, band), 0)
    tk = lax.broadcasted_iota(jnp.int32, (Q_BLOCK, band), 1)
    q_chunk = tq // CHUNK
    k_chunk = tk // CHUNK
    in_band = (k_chunk >= q_chunk) & (k_chunk <= q_chunk + LEFT_CHUNKS)
    for h in range(HEADS):
        rows = jnp.broadcast_to(tab_ref[h:h + 1, :], (Q_BLOCK, band + Q_BLOCK))
        toeplitz = pltpu.roll(rows, 0, 1, stride=1, stride_axis=0)
        o_ref[h] = jnp.where(in_band, toeplitz[:, Q_BLOCK:] * LOG2_E, MASKED)


def _band_bias(rel_bias):
    band = KEY_BLOCKS * Q_BLOCK
    offset = jnp.arange(band + Q_BLOCK) - Q_BLOCK
    rel = (KEY_BLOCKS - 1) * Q_BLOCK - offset
    table = rel_bias.astype(F32)[:, jnp.clip(rel, -REL_CLIP, REL_CLIP) + REL_CLIP]
    return pl.pallas_call(
        _band_bias_kernel,
        out_shape=jax.ShapeDtypeStruct((HEADS, Q_BLOCK, band), F32),
        name="band_bias",
    )(table)


def _attention(proj, rel_bias, batch, seq):
    t = proj.shape[0]
    steps = seq // Q_BLOCK

    def kv_spec(kb):
        return pl.BlockSpec(
            (Q_BLOCK, 2 * GROUP_WIDTH),
            lambda b, i: (b * steps + jnp.maximum(i - (KEY_BLOCKS - 1) + kb, 0), 0))

    bias = _band_bias(rel_bias)
    return pl.pallas_call(
        functools.partial(_attn_kernel, scale=HEAD_DIM ** -0.5),
        grid=(batch, steps),
        in_specs=([pl.BlockSpec((Q_BLOCK, GROUP_WIDTH), lambda b, i: (b * steps + i, 2))]
                  + [kv_spec(kb) for kb in range(KEY_BLOCKS)]
                  + [pl.BlockSpec(bias.shape, lambda b, i: (0, 0, 0))]),
        out_specs=pl.BlockSpec((Q_BLOCK, GROUP_WIDTH), lambda b, i: (b * steps + i, 0)),
        out_shape=jax.ShapeDtypeStruct((t, GROUP_WIDTH), BF16),
        compiler_params=_params("parallel", "arbitrary"),
        name="chunk_attn",
    )(proj, *([proj] * KEY_BLOCKS), bias)


def _ffn_up_kernel(a_ref, wg_ref, wv_ref, cwg_ref, cwv_ref, cbg_ref, cbv_ref, o_ref,
                   ug0_ref, uv0_ref, ug1_ref, uv1_ref, wg_bf16_ref, wv_bf16_ref, *,
                   n_tiles, row_blocks, blocks_per_seq, tm):
    s = pl.program_id(0)
    slabs = ug0_ref.shape[0]

    @pl.when(s == 0)
    def _():
        ug1_ref[...] = jnp.zeros_like(ug1_ref)
        uv1_ref[...] = jnp.zeros_like(uv1_ref)

    row_block = jnp.minimum(s, n_tiles * row_blocks - 1) % row_blocks

    @pl.when(row_block == 0)
    def _():
        wg_bf16_ref[...] = wg_ref[...].astype(BF16)
        wv_bf16_ref[...] = wv_ref[...].astype(BF16)

    def step(new, old):
        first = (row_block % blocks_per_seq) == 0
        a = a_ref[...]
        for u_new, u_old, w_ref in ((new[0], old[0], wg_bf16_ref),
                                    (new[1], old[1], wv_bf16_ref)):
            u = jnp.dot(a, w_ref[...], preferred_element_type=F32)
            for k in range(slabs):
                tail = u_old[k, pl.ds(tm, HALO), :]
                u_new[k, 0:HALO, :] = jnp.where(first, jnp.zeros_like(tail), tail)
                u_new[k, pl.ds(HALO, tm), :] = u[:, k * LANES:(k + 1) * LANES]

        zero = jnp.zeros((EPILOGUE_ROWS, LANES), F32)
        for k in range(slabs):
            lanes = slice(k * LANES, (k + 1) * LANES)
            for r in range(0, tm, EPILOGUE_ROWS):
                def causal_conv(u_ref, cw_ref, cb_ref, out):
                    for j in range(CONV_WIDTH):
                        start = r + HALO - (CONV_WIDTH - 1) + j
                        out = out + cw_ref[j:j + 1, lanes] * u_ref[
                            k, pl.ds(start, EPILOGUE_ROWS, stride=1), :]
                    return out

                gate = causal_conv(old[0], cwg_ref, cbg_ref, cbg_ref[:, lanes] + zero)
                val = causal_conv(old[1], cwv_ref, cbv_ref, cbv_ref[:, lanes])
                act = gate * _sigmoid(gate) * val
                o_ref[r:r + EPILOGUE_ROWS, lanes] = act.astype(o_ref.dtype)
                bits = lax.bitcast_convert_type(act, jnp.uint32)
                zero = lax.bitcast_convert_type((bits >> 16) >> 16, F32)

    @pl.when(s % 2 == 0)
    def _():
        step((ug0_ref, uv0_ref), (ug1_ref, uv1_ref))

    @pl.when(s % 2 == 1)
    def _():
        step((ug1_ref, uv1_ref), (ug0_ref, uv0_ref))


def _ffn_up(a, w_up, conv_w, conv_b, seq, tm, tn):
    t, d = a.shape
    d_ff = w_up.shape[1] // 2
    tm = min(tm, seq)
    n_tiles = d_ff // tn
    row_blocks = t // tm
    last = n_tiles * row_blocks - 1
    cur = lambda s: jnp.minimum(s, last)
    prev = lambda s: jnp.maximum(s - 1, 0)
    cur_gate = lambda rows: pl.BlockSpec((rows, tn), lambda s: (0, cur(s) // row_blocks))
    cur_val = lambda rows: pl.BlockSpec((rows, tn),
                                        lambda s: (0, n_tiles + cur(s) // row_blocks))
    prev_gate = lambda rows: pl.BlockSpec((rows, tn), lambda s: (0, prev(s) // row_blocks))
    prev_val = lambda rows: pl.BlockSpec((rows, tn),
                                         lambda s: (0, n_tiles + prev(s) // row_blocks))
    u_buffer = pltpu.VMEM((tn // LANES, tm + HALO, LANES), F32)
    return pl.pallas_call(
        functools.partial(_ffn_up_kernel, n_tiles=n_tiles, row_blocks=row_blocks,
                          blocks_per_seq=seq // tm, tm=tm),
        grid=(last + 2,),
        in_specs=[pl.BlockSpec((tm, d), lambda s: (cur(s) % row_blocks, 0)),
                  cur_gate(d), cur_val(d),
                  prev_gate(CONV_WIDTH), prev_val(CONV_WIDTH),
                  prev_gate(1), prev_val(1)],
        out_specs=pl.BlockSpec((tm, tn),
                               lambda s: (prev(s) % row_blocks, prev(s) // row_blocks)),
        out_shape=jax.ShapeDtypeStruct((t, d_ff), BF16),
        scratch_shapes=[u_buffer, u_buffer, u_buffer, u_buffer,
                        pltpu.VMEM((d, tn), BF16), pltpu.VMEM((d, tn), BF16)],
        compiler_params=_params("arbitrary"),
        name="ffn_up",
    )(a, w_up, w_up, conv_w, conv_w, conv_b, conv_b)


def _ffn_down_kernel(act_ref, w_ref, h_ref, o_ref):
    o_ref[...] = h_ref[...] + jnp.dot(act_ref[...], w_ref[...],
                                      preferred_element_type=F32)


def _ffn_down(act, w, h, tm, tn):
    t, d_ff = act.shape
    d = w.shape[1]
    tm = min(tm, t)
    return pl.pallas_call(
        _ffn_down_kernel,
        grid=(t // tm, d // tn),
        in_specs=[pl.BlockSpec((tm, d_ff), lambda i, j: (i, 0)),
                  pl.BlockSpec((d_ff, tn), lambda i, j: (0, j)),
                  pl.BlockSpec((tm, tn), lambda i, j: (i, j))],
        out_specs=pl.BlockSpec((tm, tn), lambda i, j: (i, j)),
        out_shape=jax.ShapeDtypeStruct((t, d), F32),
        compiler_params=_params("parallel", "arbitrary"),
        name="ffn_down",
    )(act, w, h)


def _ple_kernel(h_ref, p_ref, g_ref, wg_ref, wp_ref, fn_ref, o_ref, *, last_layer):
    h = h_ref[...]
    a = (h * _rms_scale(h) * g_ref[...]).astype(BF16)
    gate = _sigmoid(jnp.dot(a, wg_ref[...], preferred_element_type=F32))
    emb = jnp.dot(p_ref[...].astype(BF16), wp_ref[...], preferred_element_type=F32)
    h = h + gate * emb
    if last_layer:
        h = h * _rms_scale(h) * fn_ref[...]
    o_ref[...] = h


def _ple_final(h, p, g, w_gate, w_proj, final_g, last_layer, tm):
    t, d = h.shape
    tm = min(tm, t)
    whole = lambda arr: pl.BlockSpec(arr.shape, lambda i: (0, 0))
    return pl.pallas_call(
        functools.partial(_ple_kernel, last_layer=last_layer),
        grid=(t // tm,),
        in_specs=[pl.BlockSpec((tm, d), lambda i: (i, 0)),
                  pl.BlockSpec((tm, p.shape[1]), lambda i: (i, 0)),
                  whole(g), whole(w_gate), whole(w_proj), whole(final_g)],
        out_specs=pl.BlockSpec((tm, d), lambda i: (i, 0)),
        out_shape=jax.ShapeDtypeStruct((t, d), F32),
        compiler_params=_params("parallel"),
        name="ple_final",
    )(h, p, g, w_gate, w_proj, final_g)


def kernel(x, p, norm_mix, w_in, lb_logits, hg_norm, rel_bias, w_out, norm_ffn, w_up,
           conv_w, conv_b, w_down, norm_ple, w_ple_gate, w_ple_proj, final_norm):
    batch, seq, d = x.shape
    depth = w_in.shape[0]
    t = batch * seq
    hg_cols = 4 * GROUP_WIDTH
    row = lambda vec: vec.reshape(1, -1)

    h = x.reshape(t, d)
    for layer in range(depth):
        proj_hg, proj_att = _in_proj(h, row(norm_mix[layer]), w_in[layer].astype(BF16),
                                     hg_cols, tm=1024, tn=1024)
        y_att = _attention(proj_att, rel_bias[layer], batch, seq)
        h, a = _mixer_out(proj_hg, y_att, lb_logits, row(hg_norm[layer]),
                          w_out[layer].astype(BF16), h, row(norm_ffn[layer]), layer, batch,
                          seq, rows_per_step=256)
        act = _ffn_up(a, w_up[layer], conv_w[layer], row(conv_b[layer]),
                      seq, tm=1024, tn=512)
        h = _ffn_down(act, w_down[layer].astype(BF16), h, tm=1024, tn=512)
        h = _ple_final(h, p[layer].reshape(t, -1), row(norm_ple[layer]),
                       w_ple_gate[layer].astype(BF16), w_ple_proj[layer].astype(BF16),
                       row(final_norm), last_layer=layer == depth - 1, tm=512)
    return h.reshape(batch, seq, d)
```

```python
import functools

import jax
import jax.numpy as jnp
from jax import lax
from jax.experimental import pallas as pl
from jax.experimental.pallas import tpu as pltpu

F32 = jnp.float32
BF16 = jnp.bfloat16

EPS = 1e-6
CHUNK = 64
HEADS = 8
HEAD_DIM = 128
GROUP_WIDTH = HEADS * HEAD_DIM
LEFT_CHUNKS = 8
REL_CLIP = 128
CONV_WIDTH = 3
Q_BLOCK = 2 * CHUNK
KEY_BLOCKS = LEFT_CHUNKS * CHUNK // Q_BLOCK + 1
MASKED = -1e30
HALO = 8
LANES = 128
SCORES_AHEAD = 4
LEVELS = 6
LOG2_E = 1.4426950408889634
EPILOGUE_ROWS = 32
VMEM_LIMIT_BYTES = 56 * 1024 * 1024

NT_DIMS = (((1,), (1,)), ((), ()))
TN_DIMS = (((0,), (0,)), ((), ()))


def _params(*semantics, flags=None):
    return pltpu.CompilerParams(dimension_semantics=semantics,
                                vmem_limit_bytes=VMEM_LIMIT_BYTES, flags=flags)


def _rms_scale(x):
    return lax.rsqrt(jnp.mean(x * x, axis=-1, keepdims=True) + EPS)


def _sigmoid(x):
    return 0.5 * jnp.tanh(0.5 * x) + 0.5


def _silu(x):
    half = 0.5 * x
    return half + half * jnp.tanh(half)


def _in_proj_kernel(x_ref, g_ref, w_ref, hg_ref, att_ref, a_ref, *, hg_tiles):
    j = pl.program_id(1)

    @pl.when(j == 0)
    def _():
        x = x_ref[...]
        a_ref[...] = (x * _rms_scale(x) * g_ref[...]).astype(BF16)

    @pl.when(j < hg_tiles)
    def _():
        hg_ref[...] = jnp.dot(a_ref[...], w_ref[...], preferred_element_type=F32)

    @pl.when(j >= hg_tiles)
    def _():
        att_ref[...] = jnp.dot(a_ref[...], w_ref[...],
                               preferred_element_type=F32).astype(att_ref.dtype)


def _in_proj(x, g, w, hg_cols, tm, tn):
    t, d = x.shape
    n = w.shape[1]
    tm = min(tm, t)
    hg_tiles = hg_cols // tn
    att_tiles = (n - hg_cols) // tn
    assert tn == GROUP_WIDTH and att_tiles == 3
    att_tile = lambda j: (jnp.maximum(j - hg_tiles, 0) + att_tiles - 1) % att_tiles
    return pl.pallas_call(
        functools.partial(_in_proj_kernel, hg_tiles=hg_tiles),
        grid=(t // tm, n // tn),
        in_specs=[pl.BlockSpec((tm, d), lambda i, j: (i, 0)),
                  pl.BlockSpec((1, d), lambda i, j: (0, 0)),
                  pl.BlockSpec((d, tn), lambda i, j: (0, j))],
        out_specs=[pl.BlockSpec((tm, tn), lambda i, j: (i, jnp.minimum(j, hg_tiles - 1))),
                   pl.BlockSpec((tm, tn), lambda i, j: (i, att_tile(j)))],
        out_shape=[jax.ShapeDtypeStruct((t, hg_cols), F32),
                   jax.ShapeDtypeStruct((t, n - hg_cols), BF16)],
        scratch_shapes=[pltpu.VMEM((tm, d), BF16)],
        compiler_params=_params("parallel", "arbitrary"),
        name="in_proj",
    )(x, g, w)


def _select_rows(level, on_set, on_clear, bit_masks):
    n = 1 << level
    if n >= 8:
        parts = [(on_set if (r >> level) & 1 else on_clear)[r:r + 8]
                 for r in range(0, CHUNK, 8)]
        return jnp.concatenate(parts, axis=0)
    return jnp.where(bit_masks[level], on_set, on_clear)


def _swap_row_blocks(level, x, bit_masks):
    n = 1 << level
    if n >= 8:
        parts = [x[r ^ n:(r ^ n) + 8] for r in range(0, CHUNK, 8)]
        return jnp.concatenate(parts, axis=0)
    tiles = x.reshape(CHUNK // 8, 8, x.shape[-1])
    down = pltpu.roll(tiles, n, 1).reshape(x.shape)
    if 2 * n == 8:
        return down
    up = pltpu.roll(tiles, 8 - n, 1).reshape(x.shape)
    return jnp.where(bit_masks[level], down, up)


def _hgrn_level_products(q, f_pre, lb, bit_masks):
    c = 0.5 * (1.0 - lb)
    ct = c * jnp.tanh(0.5 * f_pre)
    log_f = jnp.log((lb + c) + ct) * LOG2_E
    k = c - ct
    qf = _silu(q)
    zeros = jnp.zeros_like(log_f)

    seg = log_f
    tot = log_f
    products = []
    for level in range(LEVELS):
        expo = _select_rows(level, seg, tot - seg, bit_masks)
        x = (jnp.exp2(expo) * _select_rows(level, qf, k, bit_masks)).astype(BF16)
        products.append(lax.dot_general(x, x, NT_DIMS, preferred_element_type=F32))
        sibling = _swap_row_blocks(level, tot, bit_masks)
        seg = seg + _select_rows(level, sibling, zeros, bit_masks)
        tot = tot + sibling
    return qf, k, products, seg, tot


def _hgrn_scores(products, pair_level):
    tiles = []
    for r in range(CHUNK // 8):
        rows = slice(8 * r, 8 * r + 8)
        level_of = pair_level[rows]
        acc = jnp.zeros((8, CHUNK), F32)
        for level in range(LEVELS):
            if level >= 3 and not (r >> (level - 3)) & 1:
                continue
            acc = jnp.where(level_of == level, products[level][rows], acc)
        tiles.append(acc)
    return jnp.concatenate(tiles, axis=0)


def _hgrn_chunk(q_ref, f_ref, i_ref, g_ref, y_ref, st_ref, rows, lb_all, norm_g, bit_masks,
                pair_level):
    head_cols = [slice(h * HEAD_DIM, (h + 1) * HEAD_DIM) for h in range(HEADS)]
    phase1 = [_hgrn_level_products(q_ref[rows, cols], f_ref[rows, cols],
                                   lb_all[:, cols], bit_masks) for cols in head_cols]
    outs = []
    for h, cols in enumerate(head_cols):
        qf, k, products, b, b_last = phase1[h]
        v = i_ref[rows, cols]
        vb = v.astype(BF16)
        state_t = st_ref[h]
        scores = _hgrn_scores(products, pair_level)
        diag = jnp.sum(qf * k, axis=-1, keepdims=True)
        o = jnp.dot(scores.astype(BF16), vb, preferred_element_type=F32) + diag * v
        qb = (qf * jnp.exp2(b)).astype(BF16)
        o = o + lax.dot_general(qb, state_t.astype(BF16), NT_DIMS,
                                preferred_element_type=F32)
        k_end = (k * jnp.exp2(b_last - b)).astype(BF16)
        st_ref[h] = state_t * jnp.exp2(b_last[0:1, :]) + lax.dot_general(
            vb, k_end, TN_DIMS, preferred_element_type=F32)
        outs.append(o)
    for h, cols in enumerate(head_cols):
        o = outs[h]
        g = g_ref[rows, cols]
        y = o * _rms_scale(o) * norm_g * _silu(g)
        y_ref[rows, cols] = y.astype(y_ref.dtype)


def _mixer_out_kernel(q_ref, f_ref, i_ref, g_ref, lbl_ref, ng_ref, ya_ref, x_ref, w_ref,
                      gn_ref, h_ref, a_ref, st_ref, yh0_ref, yh1_ref, w_bf16_ref, *,
                      layer, steps_per_seq, n_steps, chunks):
    step_id = pl.program_id(0)
    cur = jnp.minimum(step_id, n_steps - 1)

    @pl.when(cur % steps_per_seq == 0)
    def _():
        st_ref[...] = jnp.zeros_like(st_ref)

    @pl.when(step_id == 0)
    def _():
        yh1_ref[...] = jnp.zeros_like(yh1_ref)
        w_bf16_ref[...] = w_ref[...].astype(BF16)

    lbl = lbl_ref[...]
    e = jnp.exp(lbl - jnp.max(lbl, axis=0, keepdims=True))
    lb_all = (jnp.sum(e[0:layer + 1, :], axis=0, keepdims=True)
              / jnp.sum(e, axis=0, keepdims=True))
    norm_g = ng_ref[...]

    rows = lax.broadcasted_iota(jnp.int32, (CHUNK, HEAD_DIM), 0)
    bit_masks = [((rows >> level) & 1) == 1 for level in range(3)]
    t = lax.broadcasted_iota(jnp.int32, (CHUNK, CHUNK), 0)
    s = lax.broadcasted_iota(jnp.int32, (CHUNK, CHUNK), 1)
    pair_level = jnp.where(t > s, 31 - lax.clz(t ^ s), -1)

    d = x_ref.shape[1]
    piece = d // chunks

    def step(yh_new, yh_old):
        sum_sq = jnp.zeros((x_ref.shape[0], 1), F32)
        for c in range(chunks):
            _hgrn_chunk(q_ref, f_ref, i_ref, g_ref, yh_new, st_ref,
                        slice(c * CHUNK, (c + 1) * CHUNK), lb_all, norm_g, bit_masks,
                        pair_level)
            cols = slice(c * piece, (c + 1) * piece)
            mix = (jnp.dot(yh_old[...], w_bf16_ref[0:GROUP_WIDTH, cols],
                           preferred_element_type=F32)
                   + jnp.dot(ya_ref[...], w_bf16_ref[GROUP_WIDTH:2 * GROUP_WIDTH, cols],
                             preferred_element_type=F32))
            h = x_ref[:, cols] + mix
            h_ref[:, cols] = h
            sum_sq = sum_sq + jnp.sum(h * h, axis=-1, keepdims=True)
        scale = lax.rsqrt(sum_sq * (1.0 / d) + EPS)
        a_ref[...] = (h_ref[...] * scale * gn_ref[...]).astype(a_ref.dtype)

    @pl.when(step_id % 2 == 0)
    def _():
        step(yh0_ref, yh1_ref)

    @pl.when(step_id % 2 == 1)
    def _():
        step(yh1_ref, yh0_ref)


def _mixer_out(proj, y_att, lb_logits, norm_g, w_out, x, g_ffn, layer, batch, seq,
               rows_per_step):
    t, d = x.shape
    tb = min(rows_per_step, seq)
    steps_per_seq = seq // tb
    n_steps = batch * steps_per_seq
    cur = lambda s: jnp.minimum(s, n_steps - 1)
    prev = lambda s: jnp.maximum(s - 1, 0)
    proj_blk = lambda col: pl.BlockSpec((tb, GROUP_WIDTH), lambda s: (cur(s), col))
    prev_rows = lambda width: pl.BlockSpec((tb, width), lambda s: (prev(s), 0))
    whole = lambda arr: pl.BlockSpec(arr.shape, lambda s: (0, 0))
    return pl.pallas_call(
        functools.partial(_mixer_out_kernel, layer=layer, steps_per_seq=steps_per_seq,
                          n_steps=n_steps, chunks=tb // CHUNK),
        grid=(n_steps + 1,),
        in_specs=[proj_blk(0), proj_blk(1), proj_blk(2), proj_blk(3),
                  whole(lb_logits), whole(norm_g),
                  prev_rows(GROUP_WIDTH), prev_rows(d), whole(w_out), whole(g_ffn)],
        out_specs=[prev_rows(d), prev_rows(d)],
        out_shape=[jax.ShapeDtypeStruct((t, d), F32), jax.ShapeDtypeStruct((t, d), BF16)],
        scratch_shapes=[pltpu.VMEM((HEADS, HEAD_DIM, HEAD_DIM), F32),
                        pltpu.VMEM((tb, GROUP_WIDTH), BF16),
                        pltpu.VMEM((tb, GROUP_WIDTH), BF16),
                        pltpu.VMEM(w_out.shape, BF16)],
        compiler_params=_params("arbitrary"),
        name="hgrn2_out_proj",
    )(proj, proj, proj, proj, lb_logits, norm_g, y_att, x, w_out, g_ffn)


def _attn_kernel(q_ref, *refs, scale):
    kv_refs = refs[:KEY_BLOCKS]
    bias_ref, o_ref = refs[KEY_BLOCKS:]
    i = pl.program_id(1)
    head_cols = [slice(h * HEAD_DIM, (h + 1) * HEAD_DIM) for h in range(HEADS)]

    def attend(penalty):
        def scores_of(h):
            cols = head_cols[h]
            q = (q_ref[:, cols].astype(F32) * (scale * LOG2_E)).astype(BF16)
            k_band = jnp.concatenate([kv_ref[:, cols] for kv_ref in kv_refs], axis=0)
            s = lax.dot_general(q, k_band, NT_DIMS, preferred_element_type=F32)
            s = s + bias_ref[h]
            return s if penalty is None else s + penalty

        def finish(h, s):
            cols = head_cols[h]
            p = jnp.exp2(s - jnp.max(s, axis=-1, keepdims=True))
            denom = jnp.sum(p, axis=-1, keepdims=True)
            v_cols = slice(GROUP_WIDTH + cols.start, GROUP_WIDTH + cols.stop)
            v_band = jnp.concatenate([kv_ref[:, v_cols] for kv_ref in kv_refs], axis=0)
            o = jnp.dot(p.astype(BF16), v_band, preferred_element_type=F32)
            o_ref[:, cols] = (o / denom).astype(o_ref.dtype)

        pending = [scores_of(h) for h in range(SCORES_AHEAD)]
        for h in range(HEADS):
            if h + SCORES_AHEAD < HEADS:
                pending.append(scores_of(h + SCORES_AHEAD))
            finish(h, pending[h])

    @pl.when(i >= KEY_BLOCKS - 1)
    def _():
        attend(None)

    @pl.when(i < KEY_BLOCKS - 1)
    def _():
        attend(jnp.concatenate(
            [jnp.full((1, Q_BLOCK), jnp.where(i - (KEY_BLOCKS - 1) + kb >= 0, 0.0, MASKED), F32)
             for kb in range(KEY_BLOCKS)], axis=1))


def _band_bias_kernel(tab_ref, o_ref):
    band = KEY_BLOCKS * Q_BLOCK
    tq = lax.broadcasted_iota(jnp.int32, (Q_BLOCK, band), 0)
    tk = lax.broadcasted_iota(jnp.int32, (Q_BLOCK, band), 1)
    q_chunk = tq // CHUNK
    k_chunk = tk // CHUNK
    in_band = (k_chunk >= q_chunk) & (k_chunk <= q_chunk + LEFT_CHUNKS)
    for h in range(HEADS):
        rows = jnp.broadcast_to(tab_ref[h:h + 1, :], (Q_BLOCK, band + Q_BLOCK))
        toeplitz = pltpu.roll(rows, 0, 1, stride=1, stride_axis=0)
        o_ref[h] = jnp.where(in_band, toeplitz[:, Q_BLOCK:] * LOG2_E, MASKED)


def _band_bias(rel_bias):
    band = KEY_BLOCKS * Q_BLOCK
    offset = jnp.arange(band + Q_BLOCK) - Q_BLOCK
    rel = (KEY_BLOCKS - 1) * Q_BLOCK - offset
    table = rel_bias.astype(F32)[:, jnp.clip(rel, -REL_CLIP, REL_CLIP) + REL_CLIP]
    return pl.pallas_call(
        _band_bias_kernel,
        out_shape=jax.ShapeDtypeStruct((HEADS, Q_BLOCK, band), F32),
        name="band_bias",
    )(table)


def _attention(proj, rel_bias, batch, seq):
    t = proj.shape[0]
    steps = seq // Q_BLOCK

    def kv_spec(kb):
        return pl.BlockSpec(
            (Q_BLOCK, 2 * GROUP_WIDTH),
            lambda b, i: (b * steps + jnp.maximum(i - (KEY_BLOCKS - 1) + kb, 0), 0))

    bias = _band_bias(rel_bias)
    return pl.pallas_call(
        functools.partial(_attn_kernel, scale=HEAD_DIM ** -0.5),
        grid=(batch, steps),
        in_specs=([pl.BlockSpec((Q_BLOCK, GROUP_WIDTH), lambda b, i: (b * steps + i, 2))]
                  + [kv_spec(kb) for kb in range(KEY_BLOCKS)]
                  + [pl.BlockSpec(bias.shape, lambda b, i: (0, 0, 0))]),
        out_specs=pl.BlockSpec((Q_BLOCK, GROUP_WIDTH), lambda b, i: (b * steps + i, 0)),
        out_shape=jax.ShapeDtypeStruct((t, GROUP_WIDTH), BF16),
        compiler_params=_params("parallel", "arbitrary"),
        name="chunk_attn",
    )(proj, *([proj] * KEY_BLOCKS), bias)


def _ffn_up_kernel(a_ref, wg_ref, wv_ref, cwg_ref, cwv_ref, cbg_ref, cbv_ref, o_ref,
                   ug0_ref, uv0_ref, ug1_ref, uv1_ref, wg_bf16_ref, wv_bf16_ref, *,
                   n_tiles, row_blocks, blocks_per_seq, tm):
    s = pl.program_id(0)
    slabs = ug0_ref.shape[0]

    @pl.when(s == 0)
    def _():
        ug1_ref[...] = jnp.zeros_like(ug1_ref)
        uv1_ref[...] = jnp.zeros_like(uv1_ref)

    row_block = jnp.minimum(s, n_tiles * row_blocks - 1) % row_blocks

    @pl.when(row_block == 0)
    def _():
        wg_bf16_ref[...] = wg_ref[...].astype(BF16)
        wv_bf16_ref[...] = wv_ref[...].astype(BF16)

    def step(new, old):
        first = (row_block % blocks_per_seq) == 0
        a = a_ref[...]
        for u_new, u_old, w_ref in ((new[0], old[0], wg_bf16_ref),
                                    (new[1], old[1], wv_bf16_ref)):
            u = jnp.dot(a, w_ref[...], preferred_element_type=F32)
            for k in range(slabs):
                tail = u_old[k, pl.ds(tm, HALO), :]
                u_new[k, 0:HALO, :] = jnp.where(first, jnp.zeros_like(tail), tail)
                u_new[k, pl.ds(HALO, tm), :] = u[:, k * LANES:(k + 1) * LANES]

        zero = jnp.zeros((EPILOGUE_ROWS, LANES), F32)
        for k in range(slabs):
            lanes = slice(k * LANES, (k + 1) * LANES)
            for r in range(0, tm, EPILOGUE_ROWS):
                def causal_conv(u_ref, cw_ref, cb_ref, out):
                    for j in range(CONV_WIDTH):
                        start = r + HALO - (CONV_WIDTH - 1) + j
                        out = out + cw_ref[j:j + 1, lanes] * u_ref[
                            k, pl.ds(start, EPILOGUE_ROWS, stride=1), :]
                    return out

                gate = causal_conv(old[0], cwg_ref, cbg_ref, cbg_ref[:, lanes] + zero)
                val = causal_conv(old[1], cwv_ref, cbv_ref, cbv_ref[:, lanes])
                act = _silu(gate) * val
                o_ref[r:r + EPILOGUE_ROWS, lanes] = act.astype(o_ref.dtype)
                bits = lax.bitcast_convert_type(act, jnp.uint32)
                zero = lax.bitcast_convert_type((bits >> 16) >> 16, F32)

    @pl.when(s % 2 == 0)
    def _():
        step((ug0_ref, uv0_ref), (ug1_ref, uv1_ref))

    @pl.when(s % 2 == 1)
    def _():
        step((ug1_ref, uv1_ref), (ug0_ref, uv0_ref))


def _ffn_up(a, w_up, conv_w, conv_b, seq, tm, tn):
    t, d = a.shape
    d_ff = w_up.shape[1] // 2
    tm = min(tm, seq)
    n_tiles = d_ff // tn
    row_blocks = t // tm
    last = n_tiles * row_blocks - 1
    cur = lambda s: jnp.minimum(s, last)
    prev = lambda s: jnp.maximum(s - 1, 0)
    cur_gate = lambda rows: pl.BlockSpec((rows, tn), lambda s: (0, cur(s) // row_blocks))
    cur_val = lambda rows: pl.BlockSpec((rows, tn),
                                        lambda s: (0, n_tiles + cur(s) // row_blocks))
    prev_gate = lambda rows: pl.BlockSpec((rows, tn), lambda s: (0, prev(s) // row_blocks))
    prev_val = lambda rows: pl.BlockSpec((rows, tn),
                                         lambda s: (0, n_tiles + prev(s) // row_blocks))
    u_buffer = pltpu.VMEM((tn // LANES, tm + HALO, LANES), F32)
    return pl.pallas_call(
        functools.partial(_ffn_up_kernel, n_tiles=n_tiles, row_blocks=row_blocks,
                          blocks_per_seq=seq // tm, tm=tm),
        grid=(last + 2,),
        in_specs=[pl.BlockSpec((tm, d), lambda s: (cur(s) % row_blocks, 0)),
                  cur_gate(d), cur_val(d),
                  prev_gate(CONV_WIDTH), prev_val(CONV_WIDTH),
                  prev_gate(1), prev_val(1)],
        out_specs=pl.BlockSpec((tm, tn),
                               lambda s: (prev(s) % row_blocks, prev(s) // row_blocks)),
        out_shape=jax.ShapeDtypeStruct((t, d_ff), BF16),
        scratch_shapes=[u_buffer, u_buffer, u_buffer, u_buffer,
                        pltpu.VMEM((d, tn), BF16), pltpu.VMEM((d, tn), BF16)],
        compiler_params=_params("arbitrary"),
        name="ffn_up",
    )(a, w_up, w_up, conv_w, conv_w, conv_b, conv_b)


def _ffn_down_kernel(act_ref, w_ref, h_ref, o_ref):
    o_ref[...] = h_ref[...] + jnp.dot(act_ref[...], w_ref[...],
                                      preferred_element_type=F32)


def _ffn_down(act, w, h, tm, tn):
    t, d_ff = act.shape
    d = w.shape[1]
    tm = min(tm, t)
    return pl.pallas_call(
        _ffn_down_kernel,
        grid=(t // tm, d // tn),
        in_specs=[pl.BlockSpec((tm, d_ff), lambda i, j: (i, 0)),
                  pl.BlockSpec((d_ff, tn), lambda i, j: (0, j)),
                  pl.BlockSpec((tm, tn), lambda i, j: (i, j))],
        out_specs=pl.BlockSpec((tm, tn), lambda i, j: (i, j)),
        out_shape=jax.ShapeDtypeStruct((t, d), F32),
        compiler_params=_params("parallel", "arbitrary"),
        name="ffn_down",
    )(act, w, h)


def _ple_kernel(h_ref, p_ref, g_ref, wg_ref, wp_ref, fn_ref, o_ref, wg_bf16_ref,
                wp_bf16_ref, *, last_layer):
    @pl.when(pl.program_id(0) == 0)
    def _():
        wg_bf16_ref[...] = wg_ref[...].astype(BF16)
        wp_bf16_ref[...] = wp_ref[...].astype(BF16)

    h = h_ref[...]
    a = (h * _rms_scale(h) * g_ref[...]).astype(BF16)
    gate = _sigmoid(jnp.dot(a, wg_bf16_ref[...], preferred_element_type=F32))
    emb = jnp.dot(p_ref[...].astype(BF16), wp_bf16_ref[...], preferred_element_type=F32)
    h = h + gate * emb
    if last_layer:
        h = h * _rms_scale(h) * fn_ref[...]
    o_ref[...] = h


def _ple_final(h, p, g, w_gate, w_proj, final_g, last_layer, tm):
    t, d = h.shape
    tm = min(tm, t)
    whole = lambda arr: pl.BlockSpec(arr.shape, lambda i: (0, 0))
    return pl.pallas_call(
        functools.partial(_ple_kernel, last_layer=last_layer),
        grid=(t // tm,),
        in_specs=[pl.BlockSpec((tm, d), lambda i: (i, 0)),
                  pl.BlockSpec((tm, p.shape[1]), lambda i: (i, 0)),
                  whole(g), whole(w_gate), whole(w_proj), whole(final_g)],
        out_specs=pl.BlockSpec((tm, d), lambda i: (i, 0)),
        out_shape=jax.ShapeDtypeStruct((t, d), F32),
        scratch_shapes=[pltpu.VMEM(w_gate.shape, BF16), pltpu.VMEM(w_proj.shape, BF16)],
        compiler_params=_params("arbitrary"),
        name="ple_final",
    )(h, p, g, w_gate, w_proj, final_g)


def kernel(x, p, norm_mix, w_in, lb_logits, hg_norm, rel_bias, w_out, norm_ffn, w_up,
           conv_w, conv_b, w_down, norm_ple, w_ple_gate, w_ple_proj, final_norm):
    batch, seq, d = x.shape
    depth = w_in.shape[0]
    t = batch * seq
    hg_cols = 4 * GROUP_WIDTH
    row = lambda vec: vec.reshape(1, -1)

    h = x.reshape(t, d)
    for layer in range(depth):
        proj_hg, proj_att = _in_proj(h, row(norm_mix[layer]), w_in[layer].astype(BF16),
                                     hg_cols, tm=1024, tn=1024)
        y_att = _attention(proj_att, rel_bias[layer], batch, seq)
        h, a = _mixer_out(proj_hg, y_att, lb_logits, row(hg_norm[layer]),
                          w_out[layer], h, row(norm_ffn[layer]), layer, batch,
                          seq, rows_per_step=256)
        act = _ffn_up(a, w_up[layer], conv_w[layer], row(conv_b[layer]),
                      seq, tm=1024, tn=512)
        h = _ffn_down(act, w_down[layer].astype(BF16), h, tm=1024, tn=512)
        h = _ple_final(h, p[layer].reshape(t, -1), row(norm_ple[layer]),
                       w_ple_gate[layer], w_ple_proj[layer],
                       row(final_norm), last_layer=layer == depth - 1, tm=512)
    return h.reshape(batch, seq, d)
```
